```python
import math
import jax, jax.numpy as jnp
from jax import lax
import numpy as np

D_MODEL = 2048
BATCH = 2
SEQ = 4096
DEPTH = 4
DEC_BATCH = 128
DEC_SEQ = 8
PAST_LEN = 8192
PAGE_SIZE = 128

N_MIXERS = 3
N_POOL_LAYERS = (DEPTH + 2) // 3
N_MLA_LAYERS = (DEPTH + 1) // 3
N_DIFF_LAYERS = DEPTH // 3

POOL_WINDOWS = (2, 4, 8, 16)
POOL_GROUPS = 4
POOL_GROUP_DIM = D_MODEL // POOL_GROUPS
POOL_STATE = 15

MLA_HEADS = 16
MLA_Q_LORA = 768
MLA_KV_LORA = 512
MLA_NOPE = 128
MLA_ROPE = 64
MLA_V = 128
MLA_THETA = 10000.0

DIFF_HEADS = 16
DIFF_KV_HEADS = 4
DIFF_GROUP = DIFF_HEADS // DIFF_KV_HEADS
DIFF_HEAD_DIM = 64
DIFF_ROT = DIFF_HEAD_DIM // 4
ROPE_THETA = 500000.0
DIFF_Q_COLS = DIFF_HEADS * 2 * DIFF_HEAD_DIM
DIFF_K_COLS = DIFF_KV_HEADS * 2 * DIFF_HEAD_DIM
DIFF_V_COLS = DIFF_KV_HEADS * 2 * DIFF_HEAD_DIM

FFN_DIM = 4 * D_MODEL
Q_BLOCK = 128
EPS = 1e-6

kernel_name = 'hybrid_pool_mla_diffattn_decode_step'


def rmsnorm(x, g):
    xf = x.astype(jnp.float32)
    y = xf * lax.rsqrt(jnp.mean(xf * xf, axis=-1, keepdims=True) + EPS)
    return (y * g.astype(jnp.float32)).astype(x.dtype)


def rope(x, pos, theta):
    d = x.shape[-1]
    inv = jnp.power(jnp.float32(theta), -jnp.arange(0, d, 2, dtype=jnp.float32) / d)
    ang = pos.astype(jnp.float32)[:, None] * inv[None, :]
    shp = (pos.shape[0],) + (1,) * (x.ndim - 3) + (d // 2,)
    cos = jnp.cos(ang).reshape(shp)
    sin = jnp.sin(ang).reshape(shp)
    xf = x.astype(jnp.float32)
    x1, x2 = xf[..., : d // 2], xf[..., d // 2:]
    return jnp.concatenate([x1 * cos - x2 * sin, x2 * cos + x1 * sin], axis=-1).astype(x.dtype)


def causal_mask(q_pos, k_pos):
    return q_pos[:, None] >= k_pos[None, :]


def masked_softmax(s, mask):
    return jax.nn.softmax(jnp.where(mask, s, -jnp.inf), axis=-1)


def sweep_query_blocks(fn, q, q_pos):
    b, s = q.shape[:2]
    nb = s // Q_BLOCK
    qb = jnp.moveaxis(q.reshape((b, nb, Q_BLOCK) + q.shape[2:]), 1, 0)
    pb = q_pos.reshape(nb, Q_BLOCK)
    ob = lax.map(lambda a: fn(a[0], a[1]), (qb, pb))
    return jnp.moveaxis(ob, 0, 1).reshape((b, s) + ob.shape[3:])


def pool_mixer(h, prev, pos0, w_grp, scale):
    b, s, d = h.shape
    ext = jnp.concatenate([prev, h], axis=1)
    ext32 = ext.astype(jnp.float32)
    cs = jnp.concatenate([jnp.zeros((b, 1, d), jnp.float32), jnp.cumsum(ext32, axis=1)], axis=1)
    end = cs[:, POOL_STATE + 1:]
    pos = pos0 + jnp.arange(s)
    parts = []
    for g, w in enumerate(POOL_WINDOWS):
        c0, c1 = g * POOL_GROUP_DIM, (g + 1) * POOL_GROUP_DIM
        start = cs[:, POOL_STATE + 1 - w: POOL_STATE + 1 - w + s, c0:c1]
        cnt = jnp.minimum(pos + 1, w).astype(jnp.float32)[None, :, None]
        parts.append((end[..., c0:c1] - start) / cnt)
    delta = (jnp.concatenate(parts, axis=-1) - ext32[:, POOL_STATE:]).astype(h.dtype)
    y = jnp.einsum('bsgc,gcd->bsgd', delta.reshape(b, s, POOL_GROUPS, POOL_GROUP_DIM), w_grp)
    return y.reshape(b, s, d) * scale, ext[:, -POOL_STATE:]


def mla_project(h, pos, w_dq, q_norm, w_uq, w_dkv, kv_norm, q_gain):
    b, s, _ = h.shape
    c_q = rmsnorm(h @ w_dq, q_norm)
    q = (c_q @ w_uq).reshape(b, s, MLA_HEADS, MLA_NOPE + MLA_ROPE)
    q = jnp.concatenate([q[..., :MLA_NOPE], rope(q[..., MLA_NOPE:], pos, MLA_THETA)], axis=-1)
    q = rmsnorm(q, q_gain)
    ckr = h @ w_dkv
    c_kv = rmsnorm(ckr[..., :MLA_KV_LORA], kv_norm)
    k_rope = rope(ckr[..., MLA_KV_LORA:], pos, MLA_THETA)
    return q, c_kv, k_rope


def mla_expand(c_kv, k_rope, w_ukv, k_gain):
    lead = c_kv.shape[:-1]
    kv = (c_kv @ w_ukv).reshape(lead + (MLA_HEADS, MLA_NOPE + MLA_V))
    k_nope, v = kv[..., :MLA_NOPE], kv[..., MLA_NOPE:]
    k_r = jnp.broadcast_to(k_rope[..., None, :], lead + (MLA_HEADS, MLA_ROPE))
    k = rmsnorm(jnp.concatenate([k_nope, k_r], axis=-1), k_gain)
    return k, v


def mha(q, k, v, q_pos, k_pos):
    s = jnp.einsum('bqhd,bkhd->bhqk', q, k).astype(jnp.float32) * (q.shape[-1] ** -0.5)
    p = masked_softmax(s, causal_mask(q_pos, k_pos)).astype(v.dtype)
    return jnp.einsum('bhqk,bkhd->bqhd', p, v)


def partial_rope(x, pos):
    return jnp.concatenate([rope(x[..., :DIFF_ROT], pos, ROPE_THETA), x[..., DIFF_ROT:]], axis=-1)


def diff_project(h, pos, w_qkv, q_gain, k_gain):
    b, s, _ = h.shape
    qkv = h @ w_qkv
    q = qkv[..., :DIFF_Q_COLS].reshape(b, s, DIFF_KV_HEADS, DIFF_GROUP, 2, DIFF_HEAD_DIM)
    k = qkv[..., DIFF_Q_COLS:DIFF_Q_COLS + DIFF_K_COLS].reshape(b, s, DIFF_KV_HEADS, 2, DIFF_HEAD_DIM)
    v = qkv[..., DIFF_Q_COLS + DIFF_K_COLS:].reshape(b, s, DIFF_KV_HEADS, 2 * DIFF_HEAD_DIM)
    q = partial_rope(rmsnorm(q, q_gain), pos)
    k = partial_rope(rmsnorm(k, k_gain), pos)
    return q, k, v


def diff_attend(q, k, v, lam, q_pos, k_pos):
    s = jnp.einsum('bqngcd,bkncd->bcngqk', q, k).astype(jnp.float32) * (DIFF_HEAD_DIM ** -0.5)
    p = masked_softmax(s, causal_mask(q_pos, k_pos))
    pd = (p[:, 0] - lam * p[:, 1]).astype(v.dtype)
    return jnp.einsum('bngqk,bknd->bqngd', pd, v)


def diff_output(o, o_gain, lam_init, w_o):
    b, s = o.shape[:2]
    o = rmsnorm(o.reshape(b, s, DIFF_HEADS, 2 * DIFF_HEAD_DIM), o_gain) * (1.0 - lam_init)
    return o.reshape(b, s, DIFF_HEADS * 2 * DIFF_HEAD_DIM) @ w_o


def sqrelu_mlp(h, w_in, w_out):
    return jnp.square(jax.nn.relu(h @ w_in)) @ w_out


def setup_inputs(seed: int = 0) -> dict:
    key = jax.random.key(seed)
    ks = jax.random.split(key, 40)
    cnt = [0]

    def nxt():
        cnt[0] += 1
        return ks[cnt[0] - 1]

    def nrm(shape, scale=1.0):
        return jax.random.normal(nxt(), shape, jnp.float32) * scale

    def gain(shape):
        return 1.0 + nrm(shape, 0.1)

    n_pages = PAST_LEN // PAGE_SIZE
    n_used = DEC_BATCH * n_pages
    n_phys = n_used + n_used // 4
    page_table = jax.random.permutation(nxt(), n_phys)[:n_used].reshape(DEC_BATCH, n_pages).astype(jnp.int32)
    dv = 2 * DIFF_HEAD_DIM
    return {
        'x_prompt': nrm((BATCH, SEQ, D_MODEL)),
        'x_sample': nrm((DEC_BATCH, DEC_SEQ, D_MODEL)),
        'state_pool': nrm((N_POOL_LAYERS, DEC_BATCH, POOL_STATE, D_MODEL)),
        'cache_mla_latent': nrm((N_MLA_LAYERS, n_phys, PAGE_SIZE, MLA_KV_LORA)),
        'cache_mla_krope': nrm((N_MLA_LAYERS, n_phys, PAGE_SIZE, MLA_ROPE)),
        'cache_diff_k': nrm((N_DIFF_LAYERS, n_phys, PAGE_SIZE, DIFF_KV_HEADS, dv)),
        'cache_diff_v': nrm((N_DIFF_LAYERS, n_phys, PAGE_SIZE, DIFF_KV_HEADS, dv)),
        'page_table': page_table,
        'norm_mix': gain((DEPTH, D_MODEL)),
        'norm_mlp': gain((DEPTH, D_MODEL)),
        'pool_w': nrm((N_POOL_LAYERS, POOL_GROUPS, POOL_GROUP_DIM, POOL_GROUP_DIM), POOL_GROUP_DIM ** -0.5),
        'pool_scale': gain((N_POOL_LAYERS, D_MODEL)),
        'mla_w_dq': nrm((N_MLA_LAYERS, D_MODEL, MLA_Q_LORA), D_MODEL ** -0.5),
        'mla_q_norm': gain((N_MLA_LAYERS, MLA_Q_LORA)),
        'mla_w_uq': nrm((N_MLA_LAYERS, MLA_Q_LORA, MLA_HEADS * (MLA_NOPE + MLA_ROPE)), MLA_Q_LORA ** -0.5),
        'mla_w_dkv': nrm((N_MLA_LAYERS, D_MODEL, MLA_KV_LORA + MLA_ROPE), D_MODEL ** -0.5),
        'mla_kv_norm': gain((N_MLA_LAYERS, MLA_KV_LORA)),
        'mla_w_ukv': nrm((N_MLA_LAYERS, MLA_KV_LORA, MLA_HEADS * (MLA_NOPE + MLA_V)), MLA_KV_LORA ** -0.5),
        'mla_q_gain': gain((N_MLA_LAYERS, MLA_NOPE + MLA_ROPE)),
        'mla_k_gain': gain((N_MLA_LAYERS, MLA_NOPE + MLA_ROPE)),
        'mla_w_o': nrm((N_MLA_LAYERS, MLA_HEADS * MLA_V, D_MODEL), (MLA_HEADS * MLA_V) ** -0.5),
        'diff_w_qkv': nrm((N_DIFF_LAYERS, D_MODEL, DIFF_Q_COLS + DIFF_K_COLS + DIFF_V_COLS), D_MODEL ** -0.5),
        'diff_q_gain': gain((N_DIFF_LAYERS, 2, DIFF_HEAD_DIM)),
        'diff_k_gain': gain((N_DIFF_LAYERS, 2, DIFF_HEAD_DIM)),
        'diff_lam_q': nrm((N_DIFF_LAYERS, 2, DIFF_HEAD_DIM), 0.1),
        'diff_lam_k': nrm((N_DIFF_LAYERS, 2, DIFF_HEAD_DIM), 0.1),
        'diff_o_gain': gain((N_DIFF_LAYERS, dv)),
        'diff_w_o': nrm((N_DIFF_LAYERS, DIFF_HEADS * dv, D_MODEL), (DIFF_HEADS * dv) ** -0.5),
        'mlp_w_in': nrm((DEPTH, D_MODEL, FFN_DIM), D_MODEL ** -0.5),
        'mlp_w_out': nrm((DEPTH, FFN_DIM, D_MODEL), FFN_DIM ** -0.5),
    }


def reference(x_prompt, x_sample, state_pool, cache_mla_latent, cache_mla_krope, cache_diff_k, cache_diff_v,
              page_table, norm_mix, norm_mlp, pool_w, pool_scale, mla_w_dq, mla_q_norm, mla_w_uq, mla_w_dkv,
              mla_kv_norm, mla_w_ukv, mla_q_gain, mla_k_gain, mla_w_o, diff_w_qkv, diff_q_gain, diff_k_gain,
              diff_lam_q, diff_lam_k, diff_o_gain, diff_w_o, mlp_w_in, mlp_w_out):
    b, s, d = x_prompt.shape
    db, ds, _ = x_sample.shape
    past_len = page_table.shape[1] * PAGE_SIZE
    pos_p = jnp.arange(s)
    pos_s = past_len + jnp.arange(ds)
    kpos_s = jnp.arange(past_len + ds)
    xp, xs = x_prompt, x_sample
    pool_p, pool_s, lat_p, lat_s, kr_p, kr_s, dk_p, dk_s, dv_p, dv_s = ([] for _ in range(10))

    for i in range(DEPTH):
        kind, j = i % N_MIXERS, i // N_MIXERS
        hp = rmsnorm(xp, norm_mix[i])
        hs = rmsnorm(xs, norm_mix[i])
        if kind == 0:
            yp, stp = pool_mixer(hp, jnp.zeros((b, POOL_STATE, d), hp.dtype), 0, pool_w[j], pool_scale[j])
            ys, sts = pool_mixer(hs, state_pool[j], past_len, pool_w[j], pool_scale[j])
            pool_p.append(stp)
            pool_s.append(sts)
        elif kind == 1:
            qp, cp, rp = mla_project(hp, pos_p, mla_w_dq[j], mla_q_norm[j], mla_w_uq[j], mla_w_dkv[j],
                                     mla_kv_norm[j], mla_q_gain[j])
            kp, vp = mla_expand(cp, rp, mla_w_ukv[j], mla_k_gain[j])
            op = sweep_query_blocks(lambda qb, pb: mha(qb, kp, vp, pb, pos_p), qp, pos_p)
            yp = op.reshape(b, s, MLA_HEADS * MLA_V) @ mla_w_o[j]
            qs, cs_, rs = mla_project(hs, pos_s, mla_w_dq[j], mla_q_norm[j], mla_w_uq[j], mla_w_dkv[j],
                                      mla_kv_norm[j], mla_q_gain[j])

            def mla_seq(a):
                pt, q1, c1, r1 = a
                lat = jnp.concatenate([cache_mla_latent[j, pt].reshape(past_len, MLA_KV_LORA), c1], axis=0)
                kr = jnp.concatenate([cache_mla_krope[j, pt].reshape(past_len, MLA_ROPE), r1], axis=0)
                k1, v1 = mla_expand(lat, kr, mla_w_ukv[j], mla_k_gain[j])
                return mha(q1[None], k1[None], v1[None], pos_s, kpos_s)[0]

            os_ = lax.map(mla_seq, (page_table, qs, cs_, rs))
            ys = os_.reshape(db, ds, MLA_HEADS * MLA_V) @ mla_w_o[j]
            lat_p.append(cp)
            lat_s.append(cs_)
            kr_p.append(rp)
            kr_s.append(rs)
        else:
            lam_init = 0.8 - 0.6 * math.exp(-0.3 * i)
            lq = diff_lam_q[j].astype(jnp.float32)
            lk = diff_lam_k[j].astype(jnp.float32)
            lam = jnp.exp(jnp.sum(lq[0] * lk[0])) - jnp.exp(jnp.sum(lq[1] * lk[1])) + lam_init
            qp, kp, vp = diff_project(hp, pos_p, diff_w_qkv[j], diff_q_gain[j], diff_k_gain[j])
            op = sweep_query_blocks(lambda qb, pb: diff_attend(qb, kp, vp, lam, pb, pos_p), qp, pos_p)
            yp = diff_output(op, diff_o_gain[j], lam_init, diff_w_o[j])
            qs, ks, vs = diff_project(hs, pos_s, diff_w_qkv[j], diff_q_gain[j], diff_k_gain[j])

            def diff_seq(a):
                pt, q1, k1, v1 = a
                kc = cache_diff_k[j, pt].reshape(past_len, DIFF_KV_HEADS, 2, DIFF_HEAD_DIM)
                vc = cache_diff_v[j, pt].reshape(past_len, DIFF_KV_HEADS, 2 * DIFF_HEAD_DIM)
                kk = jnp.concatenate([kc, k1], axis=0)
                vv = jnp.concatenate([vc, v1], axis=0)
                return diff_attend(q1[None], kk[None], vv[None], lam, pos_s, kpos_s)[0]

            os_ = lax.map(diff_seq, (page_table, qs, ks, vs))
            ys = diff_output(os_, diff_o_gain[j], lam_init, diff_w_o[j])
            dk_p.append(kp.reshape(b, s, DIFF_KV_HEADS, 2 * DIFF_HEAD_DIM))
            dk_s.append(ks.reshape(db, ds, DIFF_KV_HEADS, 2 * DIFF_HEAD_DIM))
            dv_p.append(vp)
            dv_s.append(vs)
        xp = xp + yp
        xs = xs + ys
        xp = xp + sqrelu_mlp(rmsnorm(xp, norm_mlp[i]), mlp_w_in[i], mlp_w_out[i])
        xs = xs + sqrelu_mlp(rmsnorm(xs, norm_mlp[i]), mlp_w_in[i], mlp_w_out[i])

    return (xp, xs, jnp.stack(pool_p), jnp.stack(pool_s), jnp.stack(lat_p), jnp.stack(lat_s),
            jnp.stack(kr_p), jnp.stack(kr_s), jnp.stack(dk_p), jnp.stack(dk_s), jnp.stack(dv_p), jnp.stack(dv_s))
```

```python
import functools
import math

import jax
import jax.numpy as jnp
from jax import lax
from jax.experimental import pallas as pl
from jax.experimental.pallas import tpu as pltpu

F32 = jnp.float32
BF16 = jnp.bfloat16

EPS = 1e-6
POOL_WINDOWS = (2, 4, 8, 16)
POOL_STATE = 15
POOL_HALO = 16
MLA_HEADS = 16
MLA_NOPE = 128
MLA_ROPE = 64
MLA_V = 128
MLA_QK = MLA_NOPE + MLA_ROPE
MLA_THETA = 10000.0
DIFF_HEADS = 16
DIFF_KV_HEADS = 4
DIFF_GROUP = DIFF_HEADS // DIFF_KV_HEADS
DIFF_HEAD_DIM = 64
DIFF_ROT = DIFF_HEAD_DIM // 4
DIFF_THETA = 500000.0
N_MIXERS = 3

LANES = 128
SUBLANES = 8
VMEM_LIMIT_BYTES = 56 * 1024 * 1024

ROW_TILE = 512
POOL_TILE = 256
FFN_TILE = 1024
MLA_FLASH_TILE = 1024
DIFF_FLASH_TILE = 512
PAGES_PER_STEP = 8
CHUNK_PAGES = 2


def _tile(n, pref, mult=16):
    if n <= pref:
        return n
    for t in range(pref, 0, -1):
        if n % t == 0 and t % mult == 0:
            return t
    raise ValueError(f"no tile for {n} (pref {pref}, mult {mult})")


def _params(sem):
    return pltpu.CompilerParams(dimension_semantics=sem, vmem_limit_bytes=VMEM_LIMIT_BYTES)


def _rms(x, g, n=None):
    ms = jnp.sum(x * x, axis=-1, keepdims=True) * (1.0 / (n or x.shape[-1]))
    return x * lax.rsqrt(ms + EPS) * g


def _dot(a, b):
    return jnp.dot(a, b, preferred_element_type=F32)


def _dot_nt(a, b):
    return lax.dot_general(a, b, (((1,), (1,)), ((), ())), preferred_element_type=F32)


def _swap_halves(r):
    half = r.shape[-1] // 2
    return jnp.concatenate([r[:, half:], r[:, :half]], axis=-1)


def _mlp_kernel(x_ref, g_ref, win_ref, wout_ref, o_ref, h_scr, acc_scr):
    f = pl.program_id(1)

    @pl.when(f == 0)
    def _():
        h_scr[...] = _rms(x_ref[...], g_ref[...]).astype(BF16)
        acc_scr[...] = jnp.zeros_like(acc_scr)

    a = jnp.maximum(_dot(h_scr[...], win_ref[...]), 0.0)
    acc_scr[...] += _dot((a * a).astype(BF16), wout_ref[...])

    @pl.when(f == pl.num_programs(1) - 1)
    def _():
        o_ref[...] = x_ref[...] + acc_scr[...]


def _mlp(x, g, w_in, w_out):
    nt, d = x.shape
    ffn = w_in.shape[1]
    tm, tf = _tile(nt, ROW_TILE), _tile(ffn, FFN_TILE, LANES)
    return pl.pallas_call(
        _mlp_kernel,
        out_shape=jax.ShapeDtypeStruct((nt, d), F32),
        grid=(nt // tm, ffn // tf),
        in_specs=[
            pl.BlockSpec((tm, d), lambda i, f: (i, 0)),
            pl.BlockSpec((1, d), lambda i, f: (0, 0)),
            pl.BlockSpec((d, tf), lambda i, f: (0, f)),
            pl.BlockSpec((tf, d), lambda i, f: (f, 0)),
        ],
        out_specs=pl.BlockSpec((tm, d), lambda i, f: (i, 0)),
        scratch_shapes=[pltpu.VMEM((tm, d), BF16), pltpu.VMEM((tm, d), F32)],
        compiler_params=_params(("parallel", "arbitrary")),
        name="mlp",
    )(x, g.reshape(1, d), w_in, w_out)


def _proj_kernel(x_ref, a_ref, w_ref, o_ref):
    o_ref[...] = x_ref[...] + _dot(a_ref[...], w_ref[...])


def _proj_residual(x, a, w):
    nt, d = x.shape
    k = a.shape[1]
    tm = _tile(nt, ROW_TILE)
    return pl.pallas_call(
        _proj_kernel,
        out_shape=jax.ShapeDtypeStruct((nt, d), F32),
        grid=(nt // tm,),
        in_specs=[
            pl.BlockSpec((tm, d), lambda i: (i, 0)),
            pl.BlockSpec((tm, k), lambda i: (i, 0)),
            pl.BlockSpec((k, d), lambda i: (0, 0)),
        ],
        out_specs=pl.BlockSpec((tm, d), lambda i: (i, 0)),
        compiler_params=_params(("parallel",)),
        name="proj_residual",
    )(x, a, w)


def _pool_kernel(x_ref, halo_ref, st_ref, g_ref, w_ref, sc_ref, o_ref, tail_ref, sto_ref, ext_scr, exts_scr,
                 *, n_ptiles, tiles_per_seq, dec_seq, past_len):
    i = pl.program_id(0)
    t = x_ref.shape[0]
    gdim = w_ref.shape[1]
    g = g_ref[...]
    xm = x_ref[...]
    hm = _rms(xm, g)
    row = lax.broadcasted_iota(jnp.int32, (t, 1), 0)

    def mix(gi, wsum, cnt):
        c0 = gi * gdim
        delta = wsum / cnt - hm[:, c0:c0 + gdim]
        y = _dot(delta.astype(BF16), w_ref[gi]) * sc_ref[:, c0:c0 + gdim]
        o_ref[:, c0:c0 + gdim] = xm[:, c0:c0 + gdim] + y

    @pl.when(i < n_ptiles)
    def _():
        first = (i % tiles_per_seq) == 0
        hh = _rms(halo_ref[...], g)
        ext_scr[0:POOL_HALO, :] = jnp.where(first, 0.0, hh)
        ext_scr[POOL_HALO:POOL_HALO + t, :] = hm
        tail_ref[0] = hm[t - POOL_HALO:t, :]
        pos = (i % tiles_per_seq) * t + row
        for gi, w in enumerate(POOL_WINDOWS):
            c0 = gi * gdim
            wsum = hm[:, c0:c0 + gdim]
            for k in range(1, w):
                wsum = wsum + ext_scr[POOL_HALO - k:POOL_HALO - k + t, c0:c0 + gdim]
            mix(gi, wsum, jnp.minimum(pos + 1, w).astype(F32))

    @pl.when(i >= n_ptiles)
    def _():
        nseq = t // dec_seq
        hm3 = hm.reshape(nseq, dec_seq, hm.shape[1])
        exts_scr[:, 0:POOL_HALO, :] = st_ref[...]
        exts_scr[:, POOL_HALO:POOL_HALO + dec_seq, :] = hm3
        sto_ref[...] = exts_scr[:, dec_seq:dec_seq + POOL_HALO, :]
        pos = past_len + row % dec_seq
        for gi, w in enumerate(POOL_WINDOWS):
            c0 = gi * gdim
            wsum = hm3[:, :, c0:c0 + gdim]
            for k in range(1, w):
                wsum = wsum + exts_scr[:, POOL_HALO - k:POOL_HALO - k + dec_seq, c0:c0 + gdim]
            mix(gi, wsum.reshape(t, gdim), jnp.minimum(pos + 1, w).astype(F32))


def _pool_layer(x, state, g, w_grp, scale, *, batch, seq, dec_batch, dec_seq, past_len):
    nt, d = x.shape
    assert dec_seq == SUBLANES and POOL_HALO == 2 * dec_seq
    n_p, n_s = batch * seq, dec_batch * dec_seq
    t = _tile(math.gcd(seq, n_s), POOL_TILE)
    n_ptiles, n_stiles, tps = n_p // t, n_s // t, seq // t
    nseq = t // dec_seq
    state16 = jnp.pad(state, ((0, 0), (POOL_HALO - POOL_STATE, 0), (0, 0)))
    hb = t // POOL_HALO
    kern = functools.partial(_pool_kernel, n_ptiles=n_ptiles, tiles_per_seq=tps, dec_seq=dec_seq, past_len=past_len)
    xo, tail, sto = pl.pallas_call(
        kern,
        out_shape=(
            jax.ShapeDtypeStruct((nt, d), F32),
            jax.ShapeDtypeStruct((batch, POOL_HALO, d), F32),
            jax.ShapeDtypeStruct((dec_batch, POOL_HALO, d), F32),
        ),
        grid=(n_ptiles + n_stiles,),
        in_specs=[
            pl.BlockSpec((t, d), lambda i: (i, 0)),
            pl.BlockSpec((POOL_HALO, d), lambda i: (jnp.maximum(i * hb - 1, 0), 0)),
            pl.BlockSpec((nseq, POOL_HALO, d), lambda i: (jnp.maximum(i - n_ptiles, 0), 0, 0)),
            pl.BlockSpec((1, d), lambda i: (0, 0)),
            pl.BlockSpec(w_grp.shape, lambda i: (0, 0, 0)),
            pl.BlockSpec((1, d), lambda i: (0, 0)),
        ],
        out_specs=(
            pl.BlockSpec((t, d), lambda i: (i, 0)),
            pl.BlockSpec((1, POOL_HALO, d), lambda i: (jnp.minimum(i // tps, batch - 1), 0, 0)),
            pl.BlockSpec((nseq, POOL_HALO, d), lambda i: (jnp.maximum(i - n_ptiles, 0), 0, 0)),
        ),
        scratch_shapes=[pltpu.VMEM((t + POOL_HALO, d), F32), pltpu.VMEM((nseq, POOL_HALO + dec_seq, d), F32)],
        compiler_params=_params(("arbitrary",)),
        name="pool_mixer",
    )(x, x, state16, g.reshape(1, d), w_grp, scale.reshape(1, d))
    return xo, tail[:, 1:], sto[:, 1:]


def _mla_down_kernel(x_ref, g_ref, wdq_ref, wdkv_ref, qn_ref, kvn_ref, cos_ref, sin_ref, cq_ref, lat_ref, kr_ref):
    h = _rms(x_ref[...], g_ref[...]).astype(BF16)
    cq_ref[...] = _rms(_dot(h, wdq_ref[...]), qn_ref[...])
    ckr = _dot(h, wdkv_ref[...])
    lora = lat_ref.shape[1]
    lat_ref[...] = _rms(ckr[:, :lora], kvn_ref[...])
    r = ckr[:, lora:]
    kr_ref[...] = r * cos_ref[...] + _swap_halves(r) * sin_ref[...]


def _mla_down(x, g, w_dq, w_dkv, q_norm, kv_norm, cos, sin):
    nt, d = x.shape
    ql, kvl = w_dq.shape[1], kv_norm.shape[0]
    tm = _tile(nt, ROW_TILE)
    row = lambda i: (i, 0)
    fixed = lambda i: (0, 0)
    return pl.pallas_call(
        _mla_down_kernel,
        out_shape=(
            jax.ShapeDtypeStruct((nt, ql), F32),
            jax.ShapeDtypeStruct((nt, kvl), F32),
            jax.ShapeDtypeStruct((nt, MLA_ROPE), F32),
        ),
        grid=(nt // tm,),
        in_specs=[
            pl.BlockSpec((tm, d), row),
            pl.BlockSpec((1, d), fixed),
            pl.BlockSpec(w_dq.shape, fixed),
            pl.BlockSpec(w_dkv.shape, fixed),
            pl.BlockSpec((1, ql), fixed),
            pl.BlockSpec((1, kvl), fixed),
            pl.BlockSpec((tm, MLA_ROPE), row),
            pl.BlockSpec((tm, MLA_ROPE), row),
        ],
        out_specs=(pl.BlockSpec((tm, ql), row), pl.BlockSpec((tm, kvl), row), pl.BlockSpec((tm, MLA_ROPE), row)),
        compiler_params=_params(("parallel",)),
        name="mla_down",
    )(x, g.reshape(1, d), w_dq, w_dkv, q_norm.reshape(1, ql), kv_norm.reshape(1, kvl), cos, sin)


def _mla_q_kernel(cq_ref, w_ref, cos_ref, sin_ref, g_ref, q_ref):
    q = _dot(cq_ref[...].astype(BF16), w_ref[0])
    qn = q[:, :MLA_NOPE]
    r = q[:, MLA_NOPE:]
    rr = r * cos_ref[...] + _swap_halves(r) * sin_ref[...]
    ss = jnp.sum(qn * qn, axis=-1, keepdims=True) + jnp.sum(rr * rr, axis=-1, keepdims=True)
    inv = lax.rsqrt(ss * (1.0 / MLA_QK) + EPS) * (MLA_QK ** -0.5)
    q_ref[0, :, :MLA_NOPE] = qn * inv * g_ref[:, :MLA_NOPE]
    q_ref[0, :, MLA_NOPE:] = rr * inv * g_ref[:, MLA_NOPE:]


def _mla_q(cq, w_uq_h, cos, sin, q_gain):
    nt, ql = cq.shape
    tm = _tile(nt, ROW_TILE)
    return pl.pallas_call(
        _mla_q_kernel,
        out_shape=jax.ShapeDtypeStruct((MLA_HEADS, nt, MLA_QK), F32),
        grid=(nt // tm, MLA_HEADS),
        in_specs=[
            pl.BlockSpec((tm, ql), lambda i, h: (i, 0)),
            pl.BlockSpec((1, ql, MLA_QK), lambda i, h: (h, 0, 0)),
            pl.BlockSpec((tm, MLA_ROPE), lambda i, h: (i, 0)),
            pl.BlockSpec((tm, MLA_ROPE), lambda i, h: (i, 0)),
            pl.BlockSpec((1, MLA_QK), lambda i, h: (0, 0)),
        ],
        out_specs=pl.BlockSpec((1, tm, MLA_QK), lambda i, h: (h, i, 0)),
        compiler_params=_params(("parallel", "arbitrary")),
        name="mla_q",
    )(cq, w_uq_h, cos, sin, q_gain.reshape(1, MLA_QK))


def _mla_kv_kernel(lat_ref, kr_ref, w_ref, g_ref, k_ref, v_ref):
    kv = _dot(lat_ref[...].astype(BF16), w_ref[0])
    kn = kv[:, :MLA_NOPE]
    kr = kr_ref[...]
    ss = jnp.sum(kn * kn, axis=-1, keepdims=True) + jnp.sum(kr * kr, axis=-1, keepdims=True)
    inv = lax.rsqrt(ss * (1.0 / MLA_QK) + EPS)
    k_ref[0, :, :MLA_NOPE] = (kn * inv * g_ref[:, :MLA_NOPE]).astype(BF16)
    k_ref[0, :, MLA_NOPE:] = (kr * inv * g_ref[:, MLA_NOPE:]).astype(BF16)
    v_ref[0] = kv[:, MLA_NOPE:].astype(BF16)


def _mla_kv(lat, kr, w_ukv_h, k_gain, n_p):
    kvl = lat.shape[1]
    tm = _tile(n_p, ROW_TILE)
    return pl.pallas_call(
        _mla_kv_kernel,
        out_shape=(
            jax.ShapeDtypeStruct((MLA_HEADS, n_p, MLA_QK), BF16),
            jax.ShapeDtypeStruct((MLA_HEADS, n_p, MLA_V), BF16),
        ),
        grid=(n_p // tm, MLA_HEADS),
        in_specs=[
            pl.BlockSpec((tm, kvl), lambda i, h: (i, 0)),
            pl.BlockSpec((tm, MLA_ROPE), lambda i, h: (i, 0)),
            pl.BlockSpec((1, kvl, MLA_NOPE + MLA_V), lambda i, h: (h, 0, 0)),
            pl.BlockSpec((1, MLA_QK), lambda i, h: (0, 0)),
        ],
        out_specs=(
            pl.BlockSpec((1, tm, MLA_QK), lambda i, h: (h, i, 0)),
            pl.BlockSpec((1, tm, MLA_V), lambda i, h: (h, i, 0)),
        ),
        compiler_params=_params(("parallel", "arbitrary")),
        name="mla_kv",
    )(lat, kr, w_ukv_h, k_gain.reshape(1, MLA_QK))


def _softmax_step(s, v, m_scr, l_scr, acc_scr):
    m_old = m_scr[...]
    m_new = jnp.maximum(m_old, jnp.max(s, axis=-1, keepdims=True))
    alpha = jnp.exp(m_old - m_new)
    p = jnp.exp(s - m_new)
    l_scr[...] = alpha * l_scr[...] + jnp.sum(p, axis=-1, keepdims=True)
    acc_scr[...] = alpha * acc_scr[...] + _dot(p.astype(BF16), v)
    m_scr[...] = m_new


def _causal_mask(s, q0, k0):
    qi = q0 + lax.broadcasted_iota(jnp.int32, s.shape, 0)
    ki = k0 + lax.broadcasted_iota(jnp.int32, s.shape, 1)
    return jnp.where(qi >= ki, s, -jnp.inf)


def _mla_flash_kernel(q_ref, k_ref, v_ref, o_ref, m_scr, l_scr, acc_scr):
    qi, ki = pl.program_id(2), pl.program_id(3)
    t = q_ref.shape[1]

    @pl.when(ki == 0)
    def _():
        m_scr[...] = jnp.full_like(m_scr, -jnp.inf)
        l_scr[...] = jnp.zeros_like(l_scr)
        acc_scr[...] = jnp.zeros_like(acc_scr)

    @pl.when(ki < qi)
    def _():
        s = _dot_nt(q_ref[0].astype(BF16), k_ref[0])
        _softmax_step(s, v_ref[0], m_scr, l_scr, acc_scr)

    @pl.when(ki == qi)
    def _():
        s = _causal_mask(_dot_nt(q_ref[0].astype(BF16), k_ref[0]), 0, 0)
        _softmax_step(s, v_ref[0], m_scr, l_scr, acc_scr)
        o_ref[...] = (acc_scr[...] / l_scr[...]).astype(BF16)


def _mla_flash(q, k, v, batch, seq):
    t = _tile(seq, MLA_FLASH_TILE)
    nq = seq // t
    return pl.pallas_call(
        _mla_flash_kernel,
        out_shape=jax.ShapeDtypeStruct((batch * seq, MLA_HEADS * MLA_V), BF16),
        grid=(batch, MLA_HEADS, nq, nq),
        in_specs=[
            pl.BlockSpec((1, t, MLA_QK), lambda b, h, i, j: (h, b * nq + i, 0)),
            pl.BlockSpec((1, t, MLA_QK), lambda b, h, i, j: (h, b * nq + jnp.minimum(i, j), 0)),
            pl.BlockSpec((1, t, MLA_V), lambda b, h, i, j: (h, b * nq + jnp.minimum(i, j), 0)),
        ],
        out_specs=pl.BlockSpec((t, MLA_V), lambda b, h, i, j: (b * nq + i, h)),
        scratch_shapes=[pltpu.VMEM((t, 1), F32), pltpu.VMEM((t, 1), F32), pltpu.VMEM((t, MLA_V), F32)],
        compiler_params=_params(("parallel", "parallel", "parallel", "arbitrary")),
        name="mla_flash",
    )(q, k, v)


def _mla_absorb_kernel(q_ref, w_ref, g_ref, o_ref):
    qn = (q_ref[0][:, :MLA_NOPE] * g_ref[:, :MLA_NOPE]).astype(BF16)
    o_ref[0] = _dot(qn, w_ref[0])


def _mla_absorb(q, wk_t_h, k_gain, n_p, n_s):
    kvl = wk_t_h.shape[2]
    ta = math.gcd(n_p, n_s)
    while ta > 1024:
        ta //= 2
    off = n_p // ta
    return pl.pallas_call(
        _mla_absorb_kernel,
        out_shape=jax.ShapeDtypeStruct((MLA_HEADS, n_s, kvl), F32),
        grid=(MLA_HEADS, n_s // ta),
        in_specs=[
            pl.BlockSpec((1, ta, MLA_QK), lambda h, i: (h, off + i, 0)),
            pl.BlockSpec((1, MLA_NOPE, kvl), lambda h, i: (h, 0, 0)),
            pl.BlockSpec((1, MLA_QK), lambda h, i: (0, 0)),
        ],
        out_specs=pl.BlockSpec((1, ta, kvl), lambda h, i: (h, i, 0)),
        compiler_params=_params(("parallel", "arbitrary")),
        name="mla_absorb_q",
    )(q, wk_t_h, k_gain.reshape(1, MLA_QK))


def _mla_decode_kernel(pt_ref, *refs, pps, dec_seq):
    lat_refs = refs[:pps]
    kr_refs = refs[pps:2 * pps]
    wkt_ref, qabs_ref, q_ref, nlat_ref, nkr_ref, g_ref, o_ref, lhs_scr, qr_scr, m_scr, l_scr, acc_scr = refs[2 * pps:]
    n, p = pl.program_id(0), pl.program_id(1)
    nk = wkt_ref.shape[0]
    rows = MLA_HEADS * dec_seq

    @pl.when((n == 0) & (p == 0))
    def _():
        lhs_scr[0:nk, :] = wkt_ref[...]

    @pl.when(p == 0)
    def _():
        lhs_scr[nk:nk + rows, :] = qabs_ref[...].reshape(rows, qabs_ref.shape[2]).astype(BF16)
        qr = q_ref[...][:, :, MLA_NOPE:].reshape(rows, MLA_ROPE)
        qr_scr[...] = (qr * g_ref[:, MLA_NOPE:]).astype(BF16)
        m_scr[...] = jnp.full_like(m_scr, -jnp.inf)
        l_scr[...] = jnp.zeros_like(l_scr)
        acc_scr[...] = jnp.zeros_like(acc_scr)

    ones = jnp.ones((SUBLANES, MLA_ROPE), BF16)

    def chunk(lat, kr, masked):
        nn = lat.shape[0]
        latb = lat.astype(BF16)
        big = _dot_nt(lhs_scr[...], latb)
        kt = big[:nk]
        ssn = jnp.sum((kt * kt).reshape(MLA_HEADS, MLA_NOPE, nn), axis=1)
        sq = kr * kr
        hi = sq.astype(BF16)
        lo = (sq - hi.astype(F32)).astype(BF16)
        sskr = (_dot_nt(ones, hi) + _dot_nt(ones, lo))[0:1]
        inv = lax.rsqrt((ssn + sskr) * (1.0 / MLA_QK) + EPS)
        inv = jnp.broadcast_to(inv[:, None, :], (MLA_HEADS, dec_seq, nn)).reshape(rows, nn)
        s = (big[nk:] + _dot_nt(qr_scr[...], kr.astype(BF16))) * inv
        if masked:
            qpos = lax.broadcasted_iota(jnp.int32, s.shape, 0) % dec_seq
            s = jnp.where(lax.broadcasted_iota(jnp.int32, s.shape, 1) <= qpos, s, -jnp.inf)
        _softmax_step(s, latb, m_scr, l_scr, acc_scr)

    for c in range(0, pps, CHUNK_PAGES):
        cp = min(CHUNK_PAGES, pps - c)
        chunk(jnp.concatenate([lat_refs[c + j][0] for j in range(cp)], axis=0),
              jnp.concatenate([kr_refs[c + j][0] for j in range(cp)], axis=0), False)

    @pl.when(p == pl.num_programs(1) - 1)
    def _():
        pad = LANES - dec_seq
        chunk(jnp.concatenate([nlat_ref[...], jnp.zeros((pad, nlat_ref.shape[1]), F32)], axis=0),
              jnp.concatenate([nkr_ref[...], jnp.zeros((pad, MLA_ROPE), F32)], axis=0), True)
        o_ref[0] = acc_scr[...] / l_scr[...]


def _mla_decode(page_table, lat_cache, kr_cache, wk_t, qabs, q, lat, kr, k_gain, n_p, dec_seq):
    dec_batch, n_pages = page_table.shape
    page, kvl = lat_cache.shape[1], lat_cache.shape[2]
    assert dec_seq == SUBLANES and n_p % dec_seq == 0
    pps = _tile(n_pages, PAGES_PER_STEP, 1)
    rows = MLA_HEADS * dec_seq
    off = n_p // dec_seq

    def page_spec(width, j):
        return pl.BlockSpec((1, page, width), lambda n, p, pt: (pt[n * n_pages + p * pps + j], 0, 0))

    fixed2 = lambda n, p, pt: (0, 0)
    grid_spec = pltpu.PrefetchScalarGridSpec(
        num_scalar_prefetch=1,
        grid=(dec_batch, n_pages // pps),
        in_specs=[page_spec(kvl, j) for j in range(pps)] + [page_spec(MLA_ROPE, j) for j in range(pps)] + [
            pl.BlockSpec(wk_t.shape, fixed2),
            pl.BlockSpec((MLA_HEADS, dec_seq, kvl), lambda n, p, pt: (0, n, 0)),
            pl.BlockSpec((MLA_HEADS, dec_seq, MLA_QK), lambda n, p, pt: (0, off + n, 0)),
            pl.BlockSpec((dec_seq, kvl), lambda n, p, pt: (off + n, 0)),
            pl.BlockSpec((dec_seq, MLA_ROPE), lambda n, p, pt: (off + n, 0)),
            pl.BlockSpec((1, MLA_QK), fixed2),
        ],
        out_specs=pl.BlockSpec((1, rows, kvl), lambda n, p, pt: (n, 0, 0)),
        scratch_shapes=[
            pltpu.VMEM((wk_t.shape[0] + rows, kvl), BF16),
            pltpu.VMEM((rows, MLA_ROPE), BF16),
            pltpu.VMEM((rows, 1), F32),
            pltpu.VMEM((rows, 1), F32),
            pltpu.VMEM((rows, kvl), F32),
        ],
    )
    return pl.pallas_call(
        functools.partial(_mla_decode_kernel, pps=pps, dec_seq=dec_seq),
        out_shape=jax.ShapeDtypeStruct((dec_batch, rows, kvl), F32),
        grid_spec=grid_spec,
        compiler_params=_params(("arbitrary", "arbitrary")),
        name="mla_decode",
    )(page_table.reshape(-1), *([lat_cache] * pps), *([kr_cache] * pps), wk_t, qabs, q, lat, kr,
      k_gain.reshape(1, MLA_QK))


def _mla_vexpand_kernel(a_ref, w_ref, o_ref):
    a = a_ref[...]
    o_ref[...] = _dot(a.reshape(a.shape[0] * a.shape[2], a.shape[3]).astype(BF16), w_ref[0]).astype(BF16)


def _mla_vexpand(olat, wv_h, dec_seq):
    dec_batch, _, kvl = olat.shape
    n_s = dec_batch * dec_seq
    return pl.pallas_call(
        _mla_vexpand_kernel,
        out_shape=jax.ShapeDtypeStruct((n_s, MLA_HEADS * MLA_V), BF16),
        grid=(MLA_HEADS,),
        in_specs=[
            pl.BlockSpec((dec_batch, 1, dec_seq, kvl), lambda h: (0, h, 0, 0)),
            pl.BlockSpec((1, kvl, MLA_V), lambda h: (h, 0, 0)),
        ],
        out_specs=pl.BlockSpec((n_s, MLA_V), lambda h: (0, h)),
        compiler_params=_params(("parallel",)),
        name="mla_vexpand",
    )(olat.reshape(dec_batch, MLA_HEADS, dec_seq, kvl), wv_h)


def _diff_qkv_kernel(x_ref, g_ref, w_ref, gain_ref, c_ref, sa_ref, sb_ref, o_ref, h_scr, *, n_qk_blocks):
    j = pl.program_id(1)

    @pl.when(j == 0)
    def _():
        h_scr[...] = _rms(x_ref[...], g_ref[...]).astype(BF16)

    y = _dot(h_scr[...], w_ref[...])

    @pl.when(j < n_qk_blocks)
    def _():
        sq = y * y
        lo = lax.broadcasted_iota(jnp.int32, y.shape, 1) < DIFF_HEAD_DIM
        s_lo = jnp.sum(jnp.where(lo, sq, 0.0), axis=-1, keepdims=True)
        s_hi = jnp.sum(jnp.where(lo, 0.0, sq), axis=-1, keepdims=True)
        inv = lax.rsqrt(jnp.where(lo, s_lo, s_hi) * (1.0 / DIFF_HEAD_DIM) + EPS)
        yn = y * inv * gain_ref[0]
        half = DIFF_ROT // 2
        o_ref[...] = (yn * c_ref[...] + pltpu.roll(yn, LANES - half, 1) * sa_ref[...]
                      + pltpu.roll(yn, half, 1) * sb_ref[...])

    @pl.when(j >= n_qk_blocks)
    def _():
        o_ref[...] = y


def _diff_qkv(x, g, w_qkv, gains, c_tab, sa_tab, sb_tab):
    nt, d = x.shape
    cols = w_qkv.shape[1]
    n_blocks = cols // LANES
    n_qk_blocks = gains.shape[0]
    tm = _tile(nt, ROW_TILE)
    row = lambda i, j: (i, 0)
    return pl.pallas_call(
        functools.partial(_diff_qkv_kernel, n_qk_blocks=n_qk_blocks),
        out_shape=jax.ShapeDtypeStruct((nt, cols), F32),
        grid=(nt // tm, n_blocks),
        in_specs=[
            pl.BlockSpec((tm, d), row),
            pl.BlockSpec((1, d), lambda i, j: (0, 0)),
            pl.BlockSpec((d, LANES), lambda i, j: (0, j)),
            pl.BlockSpec((1, 1, LANES), lambda i, j: (jnp.minimum(j, n_qk_blocks - 1), 0, 0)),
            pl.BlockSpec((tm, LANES), row),
            pl.BlockSpec((tm, LANES), row),
            pl.BlockSpec((tm, LANES), row),
        ],
        out_specs=pl.BlockSpec((tm, LANES), lambda i, j: (i, j)),
        scratch_shapes=[pltpu.VMEM((tm, d), BF16)],
        compiler_params=_params(("parallel", "arbitrary")),
        name="diff_qkv",
    )(x, g.reshape(1, d), w_qkv, gains, c_tab, sa_tab, sb_tab)


def _diff_lambda(lq_ref, lk_ref, lam_init):
    e = jnp.exp(jnp.sum(lq_ref[...] * lk_ref[...], axis=-1, keepdims=True))
    return e[0:1] - e[1:2] + lam_init


def _diff_finish(a0, l0, a1, l1, lam, og, lam_init):
    o = a0 / l0 - lam * (a1 / l1)
    return _rms(o, og) * (1.0 - lam_init)


def _component_masks(x):
    lo = lax.broadcasted_iota(jnp.int32, x.shape, 1) < DIFF_HEAD_DIM
    return jnp.where(lo, x, 0.0).astype(BF16), jnp.where(lo, 0.0, x).astype(BF16)


def _diff_flash_kernel(q_ref, k_ref, v_ref, lq_ref, lk_ref, og_ref, o_ref, m_scr, l_scr, acc_scr, *, lam_init):
    qi, ki = pl.program_id(2), pl.program_id(3)

    @pl.when(ki == 0)
    def _():
        m_scr[...] = jnp.full_like(m_scr, -jnp.inf)
        l_scr[...] = jnp.zeros_like(l_scr)
        acc_scr[...] = jnp.zeros_like(acc_scr)

    def step(masked):
        kb = k_ref[...].astype(BF16)
        vb = v_ref[...].astype(BF16)
        for g in range(DIFF_GROUP):
            qs = _component_masks(q_ref[:, g * LANES:(g + 1) * LANES])
            for c in range(2):
                s = _dot_nt(qs[c], kb)
                if masked:
                    s = _causal_mask(s, 0, 0)
                _softmax_step(s, vb, m_scr.at[g, c], l_scr.at[g, c], acc_scr.at[g, c])

    @pl.when(ki < qi)
    def _():
        step(False)

    @pl.when(ki == qi)
    def _():
        step(True)
        lam = _diff_lambda(lq_ref, lk_ref, lam_init)
        for g in range(DIFF_GROUP):
            o = _diff_finish(acc_scr[g, 0], l_scr[g, 0], acc_scr[g, 1], l_scr[g, 1], lam, og_ref[...], lam_init)
            o_ref[:, g * LANES:(g + 1) * LANES] = o.astype(BF16)


def _diff_flash(qkv, lam_q, lam_k, o_gain, lam_init, batch, seq):
    t = _tile(seq, DIFF_FLASH_TILE)
    nq = seq // t
    gw = DIFF_GROUP * LANES
    kcol0 = DIFF_HEADS
    vcol0 = DIFF_HEADS + DIFF_KV_HEADS
    fixed = lambda b, n, i, j: (0, 0)
    return pl.pallas_call(
        functools.partial(_diff_flash_kernel, lam_init=lam_init),
        out_shape=jax.ShapeDtypeStruct((batch * seq, DIFF_HEADS * LANES), BF16),
        grid=(batch, DIFF_KV_HEADS, nq, nq),
        in_specs=[
            pl.BlockSpec((t, gw), lambda b, n, i, j: (b * nq + i, n)),
            pl.BlockSpec((t, LANES), lambda b, n, i, j: (b * nq + jnp.minimum(i, j), kcol0 + n)),
            pl.BlockSpec((t, LANES), lambda b, n, i, j: (b * nq + jnp.minimum(i, j), vcol0 + n)),
            pl.BlockSpec((2, DIFF_HEAD_DIM), fixed),
            pl.BlockSpec((2, DIFF_HEAD_DIM), fixed),
            pl.BlockSpec((1, LANES), fixed),
        ],
        out_specs=pl.BlockSpec((t, gw), lambda b, n, i, j: (b * nq + i, n)),
        scratch_shapes=[
            pltpu.VMEM((DIFF_GROUP, 2, t, 1), F32),
            pltpu.VMEM((DIFF_GROUP, 2, t, 1), F32),
            pltpu.VMEM((DIFF_GROUP, 2, t, LANES), F32),
        ],
        compiler_params=_params(("parallel", "parallel", "parallel", "arbitrary")),
        name="diff_flash",
    )(qkv, qkv, qkv, lam_q, lam_k, o_gain.reshape(1, LANES))


def _diff_decode_kernel(pt_ref, *refs, pps, dec_seq, lam_init):
    k_refs = refs[:pps]
    v_refs = refs[pps:2 * pps]
    new_ref, lq_ref, lk_ref, og_ref, o_ref, q_scr, m_scr, l_scr, acc_scr = refs[2 * pps:]
    p = pl.program_id(1)
    rows = 2 * DIFF_GROUP * dec_seq
    qcols = DIFF_HEADS * LANES
    kvw = DIFF_KV_HEADS * LANES

    @pl.when(p == 0)
    def _():
        for n in range(DIFF_KV_HEADS):
            parts = [_component_masks(new_ref[:, (n * DIFF_GROUP + g) * LANES:(n * DIFF_GROUP + g + 1) * LANES])
                     for g in range(DIFF_GROUP)]
            for c in range(2):
                for g in range(DIFF_GROUP):
                    r0 = (c * DIFF_GROUP + g) * dec_seq
                    q_scr[n, r0:r0 + dec_seq, :] = parts[g][c].astype(F32)
        m_scr[...] = jnp.full_like(m_scr, -jnp.inf)
        l_scr[...] = jnp.zeros_like(l_scr)
        acc_scr[...] = jnp.zeros_like(acc_scr)

    def chunk(k, v, masked):
        kb, vb = k.astype(BF16), v.astype(BF16)
        for n in range(DIFF_KV_HEADS):
            s = _dot_nt(q_scr[n].astype(BF16), kb[:, n * LANES:(n + 1) * LANES])
            if masked:
                qpos = lax.broadcasted_iota(jnp.int32, s.shape, 0) % dec_seq
                s = jnp.where(lax.broadcasted_iota(jnp.int32, s.shape, 1) <= qpos, s, -jnp.inf)
            _softmax_step(s, vb[:, n * LANES:(n + 1) * LANES], m_scr.at[n], l_scr.at[n], acc_scr.at[n])

    for c in range(0, pps, CHUNK_PAGES):
        cp = min(CHUNK_PAGES, pps - c)
        chunk(jnp.concatenate([k_refs[c + j][0] for j in range(cp)], axis=0),
              jnp.concatenate([v_refs[c + j][0] for j in range(cp)], axis=0), False)

    @pl.when(p == pl.num_programs(1) - 1)
    def _():
        zeros = jnp.zeros((LANES - dec_seq, kvw), F32)
        chunk(jnp.concatenate([new_ref[:, qcols:qcols + kvw], zeros], axis=0),
              jnp.concatenate([new_ref[:, qcols + kvw:qcols + 2 * kvw], zeros], axis=0), True)
        lam = _diff_lambda(lq_ref, lk_ref, lam_init)
        half = DIFF_GROUP * dec_seq
        for n in range(DIFF_KV_HEADS):
            o = _diff_finish(acc_scr[n, :half], l_scr[n, :half], acc_scr[n, half:], l_scr[n, half:], lam, og_ref[...],
                             lam_init)
            for g in range(DIFF_GROUP):
                col = (n * DIFF_GROUP + g) * LANES
                o_ref[:, col:col + LANES] = o[g * dec_seq:(g + 1) * dec_seq]


def _diff_decode(page_table, k_cache, v_cache, qkv, lam_q, lam_k, o_gain, lam_init, n_p, dec_seq):
    dec_batch, n_pages = page_table.shape
    page, kvw = k_cache.shape[1], k_cache.shape[2]
    assert dec_seq == SUBLANES and n_p % dec_seq == 0
    pps = _tile(n_pages, PAGES_PER_STEP, 1)
    rows = 2 * DIFF_GROUP * dec_seq
    off = n_p // dec_seq
    cols = qkv.shape[1]

    def page_spec(j):
        return pl.BlockSpec((1, page, kvw), lambda n, p, pt: (pt[n * n_pages + p * pps + j], 0, 0))

    fixed2 = lambda n, p, pt: (0, 0)
    grid_spec = pltpu.PrefetchScalarGridSpec(
        num_scalar_prefetch=1,
        grid=(dec_batch, n_pages // pps),
        in_specs=[page_spec(j) for j in range(pps)] * 2 + [
            pl.BlockSpec((dec_seq, cols), lambda n, p, pt: (off + n, 0)),
            pl.BlockSpec((2, DIFF_HEAD_DIM), fixed2),
            pl.BlockSpec((2, DIFF_HEAD_DIM), fixed2),
            pl.BlockSpec((1, LANES), fixed2),
        ],
        out_specs=pl.BlockSpec((dec_seq, DIFF_HEADS * LANES), lambda n, p, pt: (n, 0)),
        scratch_shapes=[
            pltpu.VMEM((DIFF_KV_HEADS, rows, LANES), F32),
            pltpu.VMEM((DIFF_KV_HEADS, rows, 1), F32),
            pltpu.VMEM((DIFF_KV_HEADS, rows, 1), F32),
            pltpu.VMEM((DIFF_KV_HEADS, rows, LANES), F32),
        ],
    )
    return pl.pallas_call(
        functools.partial(_diff_decode_kernel, pps=pps, dec_seq=dec_seq, lam_init=lam_init),
        out_shape=jax.ShapeDtypeStruct((dec_batch * dec_seq, DIFF_HEADS * LANES), F32),
        grid_spec=grid_spec,
        compiler_params=_params(("arbitrary", "arbitrary")),
        name="diff_decode",
    )(page_table.reshape(-1), *([k_cache] * pps), *([v_cache] * pps), qkv, lam_q, lam_k, o_gain.reshape(1, LANES))


def _angles(pos, dim, theta):
    inv = jnp.power(jnp.float32(theta), -jnp.arange(0, dim, 2, dtype=F32) / dim)
    ang = pos.astype(F32)[:, None] * inv[None, :]
    return jnp.cos(ang), jnp.sin(ang)


def _mla_rope_tables(pos):
    cos, sin = _angles(pos, MLA_ROPE, MLA_THETA)
    return jnp.concatenate([cos, cos], axis=1), jnp.concatenate([-sin, sin], axis=1)


def _diff_rope_tables(pos):
    cos, sin = _angles(pos, DIFF_ROT, DIFF_THETA)
    n, half = pos.shape[0], DIFF_ROT // 2
    rest = DIFF_HEAD_DIM - DIFF_ROT
    c = jnp.concatenate([cos, cos, jnp.ones((n, rest), F32)], axis=1)
    sa = jnp.concatenate([-sin, jnp.zeros((n, half + rest), F32)], axis=1)
    sb = jnp.concatenate([jnp.zeros((n, half), F32), sin, jnp.zeros((n, rest), F32)], axis=1)
    return tuple(jnp.concatenate([t, t], axis=1) for t in (c, sa, sb))


def kernel(x_prompt, x_sample, state_pool, cache_mla_latent, cache_mla_krope, cache_diff_k, cache_diff_v, page_table,
           norm_mix, norm_mlp, pool_w, pool_scale, mla_w_dq, mla_q_norm, mla_w_uq, mla_w_dkv, mla_kv_norm, mla_w_ukv,
           mla_q_gain, mla_k_gain, mla_w_o, diff_w_qkv, diff_q_gain, diff_k_gain, diff_lam_q, diff_lam_k, diff_o_gain,
           diff_w_o, mlp_w_in, mlp_w_out):
    b, s, d = x_prompt.shape
    db, ds, _ = x_sample.shape
    depth = norm_mix.shape[0]
    page = cache_mla_latent.shape[2]
    past_len = page_table.shape[1] * page
    n_p, n_s = b * s, db * ds
    kvl = mla_kv_norm.shape[1]

    x = jnp.concatenate([x_prompt.reshape(n_p, d), x_sample.reshape(n_s, d)], axis=0)
    pos = jnp.concatenate([jnp.tile(jnp.arange(s), b), jnp.tile(past_len + jnp.arange(ds), db)])
    mla_cos, mla_sin = _mla_rope_tables(pos)
    diff_tabs = _diff_rope_tables(pos)

    pool_p, pool_s, lat_p, lat_s, kr_p, kr_s, dk_p, dk_s, dv_p, dv_s = ([] for _ in range(10))
    for i in range(depth):
        kind, j = i % N_MIXERS, i // N_MIXERS
        if kind == 0:
            x, st_p, st_s = _pool_layer(x, state_pool[j], norm_mix[i], pool_w[j].astype(BF16), pool_scale[j],
                                        batch=b, seq=s, dec_batch=db, dec_seq=ds, past_len=past_len)
            pool_p.append(st_p)
            pool_s.append(st_s)
        elif kind == 1:
            cq, lat, kr = _mla_down(x, norm_mix[i], mla_w_dq[j].astype(BF16), mla_w_dkv[j].astype(BF16),
                                    mla_q_norm[j], mla_kv_norm[j], mla_cos, mla_sin)
            ql = cq.shape[1]
            w_uq_h = mla_w_uq[j].astype(BF16).reshape(ql, MLA_HEADS, MLA_QK).transpose(1, 0, 2)
            w_ukv = mla_w_ukv[j].astype(BF16).reshape(kvl, MLA_HEADS, MLA_NOPE + MLA_V)
            w_ukv_h = w_ukv.transpose(1, 0, 2)
            wk_t_h = w_ukv[:, :, :MLA_NOPE].transpose(1, 2, 0)
            q = _mla_q(cq, w_uq_h, mla_cos, mla_sin, mla_q_gain[j])
            k, v = _mla_kv(lat, kr, w_ukv_h, mla_k_gain[j], n_p)
            o_p = _mla_flash(q, k, v, b, s)
            qabs = _mla_absorb(q, wk_t_h, mla_k_gain[j], n_p, n_s)
            olat = _mla_decode(page_table, cache_mla_latent[j], cache_mla_krope[j],
                               wk_t_h.reshape(MLA_HEADS * MLA_NOPE, kvl), qabs, q, lat, kr, mla_k_gain[j], n_p, ds)
            o_s = _mla_vexpand(olat, w_ukv_h[:, :, MLA_NOPE:], ds)
            x = _proj_residual(x, jnp.concatenate([o_p, o_s], axis=0), mla_w_o[j].astype(BF16))
            lat_p.append(lat[:n_p].reshape(b, s, kvl))
            lat_s.append(lat[n_p:].reshape(db, ds, kvl))
            kr_p.append(kr[:n_p].reshape(b, s, MLA_ROPE))
            kr_s.append(kr[n_p:].reshape(db, ds, MLA_ROPE))
        else:
            lam_init = 0.8 - 0.6 * math.exp(-0.3 * i)
            qg = jnp.tile((diff_q_gain[j] * (DIFF_HEAD_DIM ** -0.5)).reshape(1, 1, LANES), (DIFF_HEADS, 1, 1))
            kg = jnp.tile(diff_k_gain[j].reshape(1, 1, LANES), (DIFF_KV_HEADS, 1, 1))
            qkv = _diff_qkv(x, norm_mix[i], diff_w_qkv[j].astype(BF16), jnp.concatenate([qg, kg], axis=0), *diff_tabs)
            kvw = DIFF_KV_HEADS * LANES
            o_p = _diff_flash(qkv, diff_lam_q[j], diff_lam_k[j], diff_o_gain[j], lam_init, b, s)
            o_s = _diff_decode(page_table, cache_diff_k[j].reshape(-1, page, kvw), cache_diff_v[j].reshape(-1, page, kvw),
                               qkv, diff_lam_q[j], diff_lam_k[j], diff_o_gain[j], lam_init, n_p, ds)
            x = _proj_residual(x, jnp.concatenate([o_p, o_s.astype(BF16)], axis=0), diff_w_o[j].astype(BF16))
            qc = DIFF_HEADS * LANES
            kk, vv = qkv[:, qc:qc + kvw], qkv[:, qc + kvw:]
            dk_p.append(kk[:n_p].reshape(b, s, DIFF_KV_HEADS, LANES))
            dk_s.append(kk[n_p:].reshape(db, ds, DIFF_KV_HEADS, LANES))
            dv_p.append(vv[:n_p].reshape(b, s, DIFF_KV_HEADS, LANES))
            dv_s.append(vv[n_p:].reshape(db, ds, DIFF_KV_HEADS, LANES))
        x = _mlp(x, norm_mlp[i], mlp_w_in[i].astype(BF16), mlp_w_out[i].astype(BF16))

    return (x[:n_p].reshape(b, s, d), x[n_p:].reshape(db, ds, d), jnp.stack(pool_p), jnp.stack(pool_s),
            jnp.stack(lat_p), jnp.stack(lat_s), jnp.stack(kr_p), jnp.stack(kr_s), jnp.stack(dk_p), jnp.stack(dk_s),
            jnp.stack(dv_p), jnp.stack(dv_s))
```

```python
import functools
import math

import jax
import jax.numpy as jnp
from jax import lax
from jax.experimental import pallas as pl
from jax.experimental.pallas import tpu as pltpu

F32 = jnp.float32
BF16 = jnp.bfloat16

EPS = 1e-6
POOL_WINDOWS = (2, 4, 8, 16)
POOL_STATE = 15
POOL_HALO = 16
MLA_HEADS = 16
MLA_NOPE = 128
MLA_ROPE = 64
MLA_V = 128
MLA_QK = MLA_NOPE + MLA_ROPE
MLA_THETA = 10000.0
DIFF_HEADS = 16
DIFF_KV_HEADS = 4
DIFF_GROUP = DIFF_HEADS // DIFF_KV_HEADS
DIFF_HEAD_DIM = 64
DIFF_ROT = DIFF_HEAD_DIM // 4
DIFF_THETA = 500000.0
N_MIXERS = 3

LANES = 128
SUBLANES = 8
VMEM_LIMIT_BYTES = 56 * 1024 * 1024

ROW_TILE = 512
POOL_TILE = 256
FFN_TILE = 1024
MLA_FLASH_Q_TILE = 512
MLA_FLASH_K_TILE = 2048
DIFF_FLASH_Q_TILE = 256
DIFF_FLASH_K_TILE = 1024
PAGES_PER_STEP = 8
CHUNK_PAGES = 2
HEADS_PER_STEP = 4
DIFF_QKV_BLOCK = DIFF_KV_HEADS * LANES

LOG2E = math.log2(math.e)


def _tile(n, pref, mult=16):
    if n <= pref:
        return n
    for t in range(pref, 0, -1):
        if n % t == 0 and t % mult == 0:
            return t
    raise ValueError(f"no tile for {n} (pref {pref}, mult {mult})")


def _params(sem):
    return pltpu.CompilerParams(dimension_semantics=sem, vmem_limit_bytes=VMEM_LIMIT_BYTES)


def _rms(x, g, n=None):
    ms = jnp.sum(x * x, axis=-1, keepdims=True) * (1.0 / (n or x.shape[-1]))
    return x * lax.rsqrt(ms + EPS) * g


def _dot(a, b):
    return jnp.dot(a, b, preferred_element_type=F32)


def _dot_nt(a, b):
    return lax.dot_general(a, b, (((1,), (1,)), ((), ())), preferred_element_type=F32)


def _swap_halves(r):
    half = r.shape[-1] // 2
    return jnp.concatenate([r[:, half:], r[:, :half]], axis=-1)


def _mlp_kernel(x_ref, g_ref, win_ref, wout_ref, o_ref, h_scr, acc_scr):
    f = pl.program_id(1)

    @pl.when(f == 0)
    def _():
        h_scr[...] = _rms(x_ref[...], g_ref[...]).astype(BF16)
        acc_scr[...] = jnp.zeros_like(acc_scr)

    a = jnp.maximum(_dot(h_scr[...], win_ref[0]), 0.0)
    acc_scr[...] += _dot((a * a).astype(BF16), wout_ref[0])

    @pl.when(f == pl.num_programs(1) - 1)
    def _():
        o_ref[...] = x_ref[...] + acc_scr[...]


def _mlp(x, g, w_in, w_out, layer):
    nt, d = x.shape
    ffn = w_in.shape[2]
    tm, tf = _tile(nt, ROW_TILE), _tile(ffn, FFN_TILE, LANES)
    return pl.pallas_call(
        _mlp_kernel,
        out_shape=jax.ShapeDtypeStruct((nt, d), F32),
        grid=(nt // tm, ffn // tf),
        in_specs=[
            pl.BlockSpec((tm, d), lambda i, f: (i, 0)),
            pl.BlockSpec((1, d), lambda i, f: (0, 0)),
            pl.BlockSpec((1, d, tf), lambda i, f: (layer, 0, f)),
            pl.BlockSpec((1, tf, d), lambda i, f: (layer, f, 0)),
        ],
        out_specs=pl.BlockSpec((tm, d), lambda i, f: (i, 0)),
        scratch_shapes=[pltpu.VMEM((tm, d), BF16), pltpu.VMEM((tm, d), F32)],
        compiler_params=_params(("parallel", "arbitrary")),
        name="mlp",
    )(x, g.reshape(1, d), w_in, w_out)


def _proj_kernel(x_ref, a_ref, w_ref, o_ref):
    o_ref[...] = x_ref[...] + _dot(a_ref[...], w_ref[...])


def _proj_residual(x, a, w):
    nt, d = x.shape
    k = a.shape[1]
    tm = _tile(nt, ROW_TILE)
    return pl.pallas_call(
        _proj_kernel,
        out_shape=jax.ShapeDtypeStruct((nt, d), F32),
        grid=(nt // tm,),
        in_specs=[
            pl.BlockSpec((tm, d), lambda i: (i, 0)),
            pl.BlockSpec((tm, k), lambda i: (i, 0)),
            pl.BlockSpec((k, d), lambda i: (0, 0)),
        ],
        out_specs=pl.BlockSpec((tm, d), lambda i: (i, 0)),
        compiler_params=_params(("parallel",)),
        name="proj_residual",
    )(x, a, w)


def _pool_kernel(x_ref, halo_ref, st_ref, g_ref, w_ref, sc_ref, o_ref, tail_ref, sto_ref, ext_scr, exts_scr,
                 *, n_ptiles, tiles_per_seq, dec_seq, past_len):
    i = pl.program_id(0)
    t = x_ref.shape[0]
    gdim = w_ref.shape[1]
    g = g_ref[...]
    xm = x_ref[...]
    hm = _rms(xm, g)
    row = lax.broadcasted_iota(jnp.int32, (t, 1), 0)

    def mix(gi, wsum, cnt):
        c0 = gi * gdim
        delta = wsum / cnt - hm[:, c0:c0 + gdim]
        y = _dot(delta.astype(BF16), w_ref[gi]) * sc_ref[:, c0:c0 + gdim]
        o_ref[:, c0:c0 + gdim] = xm[:, c0:c0 + gdim] + y

    @pl.when(i < n_ptiles)
    def _():
        first = (i % tiles_per_seq) == 0
        hh = _rms(halo_ref[...], g)
        ext_scr[0:POOL_HALO, :] = jnp.where(first, 0.0, hh)
        ext_scr[POOL_HALO:POOL_HALO + t, :] = hm
        tail_ref[0] = hm[t - POOL_HALO:t, :]
        pos = (i % tiles_per_seq) * t + row
        for gi, w in enumerate(POOL_WINDOWS):
            c0 = gi * gdim
            wsum = hm[:, c0:c0 + gdim]
            for k in range(1, w):
                wsum = wsum + ext_scr[POOL_HALO - k:POOL_HALO - k + t, c0:c0 + gdim]
            mix(gi, wsum, jnp.minimum(pos + 1, w).astype(F32))

    @pl.when(i >= n_ptiles)
    def _():
        nseq = t // dec_seq
        hm3 = hm.reshape(nseq, dec_seq, hm.shape[1])
        exts_scr[:, 0:POOL_HALO, :] = st_ref[...]
        exts_scr[:, POOL_HALO:POOL_HALO + dec_seq, :] = hm3
        sto_ref[...] = exts_scr[:, dec_seq:dec_seq + POOL_HALO, :]
        pos = past_len + row % dec_seq
        for gi, w in enumerate(POOL_WINDOWS):
            c0 = gi * gdim
            wsum = hm3[:, :, c0:c0 + gdim]
            for k in range(1, w):
                wsum = wsum + exts_scr[:, POOL_HALO - k:POOL_HALO - k + dec_seq, c0:c0 + gdim]
            mix(gi, wsum.reshape(t, gdim), jnp.minimum(pos + 1, w).astype(F32))


def _pool_layer(x, state, g, w_grp, scale, *, batch, seq, dec_batch, dec_seq, past_len):
    nt, d = x.shape
    assert dec_seq == SUBLANES and POOL_HALO == 2 * dec_seq
    n_p, n_s = batch * seq, dec_batch * dec_seq
    t = _tile(math.gcd(seq, n_s), POOL_TILE)
    n_ptiles, n_stiles, tps = n_p // t, n_s // t, seq // t
    nseq = t // dec_seq
    state16 = jnp.pad(state, ((0, 0), (POOL_HALO - POOL_STATE, 0), (0, 0)))
    hb = t // POOL_HALO
    kern = functools.partial(_pool_kernel, n_ptiles=n_ptiles, tiles_per_seq=tps, dec_seq=dec_seq, past_len=past_len)
    xo, tail, sto = pl.pallas_call(
        kern,
        out_shape=(
            jax.ShapeDtypeStruct((nt, d), F32),
            jax.ShapeDtypeStruct((batch, POOL_HALO, d), F32),
            jax.ShapeDtypeStruct((dec_batch, POOL_HALO, d), F32),
        ),
        grid=(n_ptiles + n_stiles,),
        in_specs=[
            pl.BlockSpec((t, d), lambda i: (i, 0)),
            pl.BlockSpec((POOL_HALO, d), lambda i: (jnp.maximum(i * hb - 1, 0), 0)),
            pl.BlockSpec((nseq, POOL_HALO, d), lambda i: (jnp.maximum(i - n_ptiles, 0), 0, 0)),
            pl.BlockSpec((1, d), lambda i: (0, 0)),
            pl.BlockSpec(w_grp.shape, lambda i: (0, 0, 0)),
            pl.BlockSpec((1, d), lambda i: (0, 0)),
        ],
        out_specs=(
            pl.BlockSpec((t, d), lambda i: (i, 0)),
            pl.BlockSpec((1, POOL_HALO, d), lambda i: (jnp.minimum(i // tps, batch - 1), 0, 0)),
            pl.BlockSpec((nseq, POOL_HALO, d), lambda i: (jnp.maximum(i - n_ptiles, 0), 0, 0)),
        ),
        scratch_shapes=[pltpu.VMEM((t + POOL_HALO, d), F32), pltpu.VMEM((nseq, POOL_HALO + dec_seq, d), F32)],
        compiler_params=_params(("arbitrary",)),
        name="pool_mixer",
    )(x, x, state16, g.reshape(1, d), w_grp, scale.reshape(1, d))
    return xo, tail[:, 1:], sto[:, 1:]


def _mla_down_kernel(x_ref, g_ref, wdq_ref, wdkv_ref, qn_ref, kvn_ref, cos_ref, sin_ref, cq_ref, lat_ref, kr_ref):
    h = _rms(x_ref[...], g_ref[...]).astype(BF16)
    cq_ref[...] = _rms(_dot(h, wdq_ref[...]), qn_ref[...])
    ckr = _dot(h, wdkv_ref[...])
    lora = lat_ref.shape[1]
    lat_ref[...] = _rms(ckr[:, :lora], kvn_ref[...])
    r = ckr[:, lora:]
    kr_ref[...] = r * cos_ref[...] + _swap_halves(r) * sin_ref[...]


def _mla_down(x, g, w_dq, w_dkv, q_norm, kv_norm, cos, sin):
    nt, d = x.shape
    ql, kvl = w_dq.shape[1], kv_norm.shape[0]
    tm = _tile(nt, ROW_TILE)
    row = lambda i: (i, 0)
    fixed = lambda i: (0, 0)
    return pl.pallas_call(
        _mla_down_kernel,
        out_shape=(
            jax.ShapeDtypeStruct((nt, ql), F32),
            jax.ShapeDtypeStruct((nt, kvl), F32),
            jax.ShapeDtypeStruct((nt, MLA_ROPE), F32),
        ),
        grid=(nt // tm,),
        in_specs=[
            pl.BlockSpec((tm, d), row),
            pl.BlockSpec((1, d), fixed),
            pl.BlockSpec(w_dq.shape, fixed),
            pl.BlockSpec(w_dkv.shape, fixed),
            pl.BlockSpec((1, ql), fixed),
            pl.BlockSpec((1, kvl), fixed),
            pl.BlockSpec((tm, MLA_ROPE), row),
            pl.BlockSpec((tm, MLA_ROPE), row),
        ],
        out_specs=(pl.BlockSpec((tm, ql), row), pl.BlockSpec((tm, kvl), row), pl.BlockSpec((tm, MLA_ROPE), row)),
        compiler_params=_params(("parallel",)),
        name="mla_down",
    )(x, g.reshape(1, d), w_dq, w_dkv, q_norm.reshape(1, ql), kv_norm.reshape(1, kvl), cos, sin)


def _mla_q_kernel(cq_ref, w_ref, cos_ref, sin_ref, g_ref, q_ref):
    cq = cq_ref[...].astype(BF16)
    for h in range(w_ref.shape[0]):
        q = _dot(cq, w_ref[h])
        qn = q[:, :MLA_NOPE]
        r = q[:, MLA_NOPE:]
        rr = r * cos_ref[...] + _swap_halves(r) * sin_ref[...]
        ss = jnp.sum(qn * qn, axis=-1, keepdims=True) + jnp.sum(rr * rr, axis=-1, keepdims=True)
        inv = lax.rsqrt(ss * (1.0 / MLA_QK) + EPS) * (MLA_QK ** -0.5 * LOG2E)
        q_ref[h, :, :MLA_NOPE] = qn * inv * g_ref[:, :MLA_NOPE]
        q_ref[h, :, MLA_NOPE:] = rr * inv * g_ref[:, MLA_NOPE:]


def _mla_q(cq, w_uq_h, cos, sin, q_gain):
    nt, ql = cq.shape
    tm = _tile(nt, ROW_TILE)
    hps = HEADS_PER_STEP
    return pl.pallas_call(
        _mla_q_kernel,
        out_shape=jax.ShapeDtypeStruct((MLA_HEADS, nt, MLA_QK), F32),
        grid=(nt // tm, MLA_HEADS // hps),
        in_specs=[
            pl.BlockSpec((tm, ql), lambda i, h: (i, 0)),
            pl.BlockSpec((hps, ql, MLA_QK), lambda i, h: (h, 0, 0)),
            pl.BlockSpec((tm, MLA_ROPE), lambda i, h: (i, 0)),
            pl.BlockSpec((tm, MLA_ROPE), lambda i, h: (i, 0)),
            pl.BlockSpec((1, MLA_QK), lambda i, h: (0, 0)),
        ],
        out_specs=pl.BlockSpec((hps, tm, MLA_QK), lambda i, h: (h, i, 0)),
        compiler_params=_params(("parallel", "arbitrary")),
        name="mla_q",
    )(cq, w_uq_h, cos, sin, q_gain.reshape(1, MLA_QK))


def _mla_kv_kernel(lat_ref, kr_ref, w_ref, g_ref, k_ref, v_ref):
    latb = lat_ref[...].astype(BF16)
    kr = kr_ref[...]
    sskr = jnp.sum(kr * kr, axis=-1, keepdims=True)
    for h in range(w_ref.shape[0]):
        kv = _dot(latb, w_ref[h])
        kn = kv[:, :MLA_NOPE]
        inv = lax.rsqrt((jnp.sum(kn * kn, axis=-1, keepdims=True) + sskr) * (1.0 / MLA_QK) + EPS)
        k_ref[h, :, :MLA_NOPE] = (kn * inv * g_ref[:, :MLA_NOPE]).astype(BF16)
        k_ref[h, :, MLA_NOPE:] = (kr * inv * g_ref[:, MLA_NOPE:]).astype(BF16)
        v_ref[h] = kv[:, MLA_NOPE:].astype(BF16)


def _mla_kv(lat, kr, w_ukv_h, k_gain, n_p):
    kvl = lat.shape[1]
    tm = _tile(n_p, ROW_TILE)
    hps = HEADS_PER_STEP
    return pl.pallas_call(
        _mla_kv_kernel,
        out_shape=(
            jax.ShapeDtypeStruct((MLA_HEADS, n_p, MLA_QK), BF16),
            jax.ShapeDtypeStruct((MLA_HEADS, n_p, MLA_V), BF16),
        ),
        grid=(n_p // tm, MLA_HEADS // hps),
        in_specs=[
            pl.BlockSpec((tm, kvl), lambda i, h: (i, 0)),
            pl.BlockSpec((tm, MLA_ROPE), lambda i, h: (i, 0)),
            pl.BlockSpec((hps, kvl, MLA_NOPE + MLA_V), lambda i, h: (h, 0, 0)),
            pl.BlockSpec((1, MLA_QK), lambda i, h: (0, 0)),
        ],
        out_specs=(
            pl.BlockSpec((hps, tm, MLA_QK), lambda i, h: (h, i, 0)),
            pl.BlockSpec((hps, tm, MLA_V), lambda i, h: (h, i, 0)),
        ),
        compiler_params=_params(("parallel", "arbitrary")),
        name="mla_kv",
    )(lat, kr, w_ukv_h, k_gain.reshape(1, MLA_QK))


def _softmax_step(s, v, m_scr, l_scr, acc_scr):
    m_old = m_scr[...]
    m_new = jnp.maximum(m_old, jnp.max(s, axis=-1, keepdims=True))
    alpha = jnp.exp2(m_old - m_new)
    p = jnp.exp2(s - m_new)
    l_scr[...] = alpha * l_scr[...] + jnp.sum(p, axis=-1, keepdims=True)
    acc_scr[...] = alpha * acc_scr[...] + _dot(p.astype(BF16), v)
    m_scr[...] = m_new


def _causal_mask(s, q0, k0):
    qi = q0 + lax.broadcasted_iota(jnp.int32, s.shape, 0)
    ki = k0 + lax.broadcasted_iota(jnp.int32, s.shape, 1)
    return jnp.where(qi >= ki, s, -jnp.inf)


def _mla_flash_kernel(q_ref, k_ref, v_ref, o_ref, m_scr, l_scr, acc_scr):
    qi, ki = pl.program_id(2), pl.program_id(3)
    tq, tk = q_ref.shape[1], k_ref.shape[1]
    last = ((qi + 1) * tq - 1) // tk
    visible = (ki + 1) * tk <= qi * tq + 1

    @pl.when(ki == 0)
    def _():
        m_scr[...] = jnp.full_like(m_scr, -jnp.inf)
        l_scr[...] = jnp.zeros_like(l_scr)
        acc_scr[...] = jnp.zeros_like(acc_scr)

    @pl.when(visible)
    def _():
        s = _dot_nt(q_ref[0].astype(BF16), k_ref[0])
        _softmax_step(s, v_ref[0], m_scr, l_scr, acc_scr)

    @pl.when(jnp.logical_not(visible) & (ki <= last))
    def _():
        s = _causal_mask(_dot_nt(q_ref[0].astype(BF16), k_ref[0]), qi * tq, ki * tk)
        _softmax_step(s, v_ref[0], m_scr, l_scr, acc_scr)

    @pl.when(ki == last)
    def _():
        o_ref[...] = (acc_scr[...] / l_scr[...]).astype(BF16)


def _mla_flash(q, k, v, batch, seq):
    tq, tk = _tile(seq, MLA_FLASH_Q_TILE), _tile(seq, MLA_FLASH_K_TILE)
    nq, nk = seq // tq, seq // tk

    def kv_rows(b, i, j):
        return b * nk + jnp.minimum(j, ((i + 1) * tq - 1) // tk)

    return pl.pallas_call(
        _mla_flash_kernel,
        out_shape=jax.ShapeDtypeStruct((batch * seq, MLA_HEADS * MLA_V), BF16),
        grid=(batch, MLA_HEADS, nq, nk),
        in_specs=[
            pl.BlockSpec((1, tq, MLA_QK), lambda b, h, i, j: (h, b * nq + i, 0)),
            pl.BlockSpec((1, tk, MLA_QK), lambda b, h, i, j: (h, kv_rows(b, i, j), 0)),
            pl.BlockSpec((1, tk, MLA_V), lambda b, h, i, j: (h, kv_rows(b, i, j), 0)),
        ],
        out_specs=pl.BlockSpec((tq, MLA_V), lambda b, h, i, j: (b * nq + i, h)),
        scratch_shapes=[pltpu.VMEM((tq, 1), F32), pltpu.VMEM((tq, 1), F32), pltpu.VMEM((tq, MLA_V), F32)],
        compiler_params=_params(("parallel", "parallel", "parallel", "arbitrary")),
        name="mla_flash",
    )(q, k, v)


def _mla_absorb_kernel(q_ref, w_ref, g_ref, o_ref):
    qn = (q_ref[0][:, :MLA_NOPE] * g_ref[:, :MLA_NOPE]).astype(BF16)
    o_ref[0] = _dot(qn, w_ref[0])


def _mla_absorb(q, wk_t_h, k_gain, n_p, n_s):
    kvl = wk_t_h.shape[2]
    ta = math.gcd(n_p, n_s)
    while ta > 1024:
        ta //= 2
    off = n_p // ta
    return pl.pallas_call(
        _mla_absorb_kernel,
        out_shape=jax.ShapeDtypeStruct((MLA_HEADS, n_s, kvl), F32),
        grid=(MLA_HEADS, n_s // ta),
        in_specs=[
            pl.BlockSpec((1, ta, MLA_QK), lambda h, i: (h, off + i, 0)),
            pl.BlockSpec((1, MLA_NOPE, kvl), lambda h, i: (h, 0, 0)),
            pl.BlockSpec((1, MLA_QK), lambda h, i: (0, 0)),
        ],
        out_specs=pl.BlockSpec((1, ta, kvl), lambda h, i: (h, i, 0)),
        compiler_params=_params(("parallel", "arbitrary")),
        name="mla_absorb_q",
    )(q, wk_t_h, k_gain.reshape(1, MLA_QK))


def _mla_decode_kernel(pt_ref, *refs, pps, dec_seq):
    lat_refs = refs[:pps]
    krt_refs = refs[pps:2 * pps]
    (wkt_ref, qabs_ref, q_ref, nlat_ref, nkr_ref, g_ref, o_ref,
     lhs_scr, qr_scr, latb_scr, big_scr, m_scr, l_scr, acc_scr) = refs[2 * pps:]
    n, p = pl.program_id(0), pl.program_id(1)
    nk = wkt_ref.shape[0]
    rows = MLA_HEADS * dec_seq
    page = lat_refs[0].shape[2]
    cn = big_scr.shape[2]
    cpages = cn // page

    @pl.when((n == 0) & (p == 0))
    def _():
        lhs_scr[0:nk, :] = wkt_ref[...]

    @pl.when(p == 0)
    def _():
        lhs_scr[nk:nk + rows, :] = qabs_ref[...].reshape(rows, qabs_ref.shape[2]).astype(BF16)
        qr = q_ref[...][:, :, MLA_NOPE:].reshape(rows, MLA_ROPE)
        qr_scr[...] = (qr * g_ref[:, MLA_NOPE:]).astype(BF16)
        m_scr[...] = jnp.full_like(m_scr, -jnp.inf)
        l_scr[...] = jnp.zeros_like(l_scr)
        acc_scr[...] = jnp.zeros_like(acc_scr)

    def scores(big, sskr, s_rope):
        nn = big.shape[1]
        kt = big[:nk]
        ssn = jnp.sum((kt * kt).reshape(MLA_HEADS, MLA_NOPE, nn), axis=1)
        inv = lax.rsqrt((ssn + sskr) * (1.0 / MLA_QK) + EPS)
        inv = jnp.broadcast_to(inv[:, None, :], (MLA_HEADS, dec_seq, nn)).reshape(rows, nn)
        return (big[nk:] + s_rope) * inv

    for j in range(pps):
        latb_scr[j * page:(j + 1) * page, :] = lat_refs[j][0, 0].astype(BF16)
    for c in range(pps // cpages):
        big_scr[c] = _dot_nt(lhs_scr[...], latb_scr[c * cn:(c + 1) * cn, :])
    parts = []
    for c in range(pps // cpages):
        krt = jnp.concatenate([krt_refs[c * cpages + j][0, 0] for j in range(cpages)], axis=1)
        sskr = jnp.sum(krt * krt, axis=0, keepdims=True)
        parts.append(scores(big_scr[c], sskr, _dot(qr_scr[...], krt.astype(BF16))))
    _softmax_step(jnp.concatenate(parts, axis=1), latb_scr[...], m_scr, l_scr, acc_scr)

    @pl.when(p == pl.num_programs(1) - 1)
    def _():
        pad = LANES - dec_seq
        latn = jnp.concatenate([nlat_ref[...], jnp.zeros((pad, nlat_ref.shape[1]), F32)], axis=0).astype(BF16)
        kr = jnp.concatenate([nkr_ref[...], jnp.zeros((pad, MLA_ROPE), F32)], axis=0)
        sq = kr * kr
        hi = sq.astype(BF16)
        lo = (sq - hi.astype(F32)).astype(BF16)
        ones = jnp.ones((SUBLANES, MLA_ROPE), BF16)
        sskr = (_dot_nt(ones, hi) + _dot_nt(ones, lo))[0:1]
        s = scores(_dot_nt(lhs_scr[...], latn), sskr, _dot_nt(qr_scr[...], kr.astype(BF16)))
        qpos = lax.broadcasted_iota(jnp.int32, s.shape, 0) % dec_seq
        s = jnp.where(lax.broadcasted_iota(jnp.int32, s.shape, 1) <= qpos, s, -jnp.inf)
        _softmax_step(s, latn, m_scr, l_scr, acc_scr)
        o_ref[0] = acc_scr[...] / l_scr[...]


def _mla_decode(page_table, layer, lat_cache, krt_cache, wk_t, qabs, q, lat, kr, k_gain, n_p, dec_seq):
    dec_batch, n_pages = page_table.shape
    page, kvl = lat_cache.shape[2], lat_cache.shape[3]
    assert dec_seq == SUBLANES and n_p % dec_seq == 0
    pps = _tile(n_pages, PAGES_PER_STEP, 1)
    cpages = _tile(pps, CHUNK_PAGES, 1)
    rows = MLA_HEADS * dec_seq
    off = n_p // dec_seq

    def page_spec(shape, j):
        return pl.BlockSpec((1, 1) + shape, lambda n, p, pt: (layer, pt[n * n_pages + p * pps + j], 0, 0))

    fixed2 = lambda n, p, pt: (0, 0)
    grid_spec = pltpu.PrefetchScalarGridSpec(
        num_scalar_prefetch=1,
        grid=(dec_batch, n_pages // pps),
        in_specs=[page_spec((page, kvl), j) for j in range(pps)] + [page_spec((MLA_ROPE, page), j) for j in range(pps)] + [
            pl.BlockSpec(wk_t.shape, fixed2),
            pl.BlockSpec((MLA_HEADS, dec_seq, kvl), lambda n, p, pt: (0, n, 0)),
            pl.BlockSpec((MLA_HEADS, dec_seq, MLA_QK), lambda n, p, pt: (0, off + n, 0)),
            pl.BlockSpec((dec_seq, kvl), lambda n, p, pt: (off + n, 0)),
            pl.BlockSpec((dec_seq, MLA_ROPE), lambda n, p, pt: (off + n, 0)),
            pl.BlockSpec((1, MLA_QK), fixed2),
        ],
        out_specs=pl.BlockSpec((1, rows, kvl), lambda n, p, pt: (n, 0, 0)),
        scratch_shapes=[
            pltpu.VMEM((wk_t.shape[0] + rows, kvl), BF16),
            pltpu.VMEM((rows, MLA_ROPE), BF16),
            pltpu.VMEM((pps * page, kvl), BF16),
            pltpu.VMEM((pps // cpages, wk_t.shape[0] + rows, cpages * page), F32),
            pltpu.VMEM((rows, 1), F32),
            pltpu.VMEM((rows, 1), F32),
            pltpu.VMEM((rows, kvl), F32),
        ],
    )
    return pl.pallas_call(
        functools.partial(_mla_decode_kernel, pps=pps, dec_seq=dec_seq),
        out_shape=jax.ShapeDtypeStruct((dec_batch, rows, kvl), F32),
        grid_spec=grid_spec,
        compiler_params=_params(("arbitrary", "arbitrary")),
        name="mla_decode",
    )(page_table.reshape(-1), *([lat_cache] * pps), *([krt_cache] * pps), wk_t, qabs, q, lat, kr,
      k_gain.reshape(1, MLA_QK))


def _mla_vexpand_kernel(a_ref, w_ref, o_ref):
    a = a_ref[...]
    o_ref[...] = _dot(a.reshape(a.shape[0] * a.shape[2], a.shape[3]).astype(BF16), w_ref[0]).astype(BF16)


def _mla_vexpand(olat, wv_h, dec_seq):
    dec_batch, _, kvl = olat.shape
    n_s = dec_batch * dec_seq
    return pl.pallas_call(
        _mla_vexpand_kernel,
        out_shape=jax.ShapeDtypeStruct((n_s, MLA_HEADS * MLA_V), BF16),
        grid=(MLA_HEADS,),
        in_specs=[
            pl.BlockSpec((dec_batch, 1, dec_seq, kvl), lambda h: (0, h, 0, 0)),
            pl.BlockSpec((1, kvl, MLA_V), lambda h: (h, 0, 0)),
        ],
        out_specs=pl.BlockSpec((n_s, MLA_V), lambda h: (0, h)),
        compiler_params=_params(("parallel",)),
        name="mla_vexpand",
    )(olat.reshape(dec_batch, MLA_HEADS, dec_seq, kvl), wv_h)


def _diff_qkv_kernel(x_ref, g_ref, w_ref, gain_ref, c_ref, sa_ref, sb_ref, o_ref, h_scr, *, n_qk_blocks):
    j = pl.program_id(1)

    @pl.when(j == 0)
    def _():
        h_scr[...] = _rms(x_ref[...], g_ref[...]).astype(BF16)

    y_all = _dot(h_scr[...], w_ref[...])

    @pl.when(j < n_qk_blocks)
    def _():
        half = DIFF_ROT // 2
        for b in range(y_all.shape[1] // LANES):
            cols = slice(b * LANES, (b + 1) * LANES)
            y = y_all[:, cols]
            sq = y * y
            lo = lax.broadcasted_iota(jnp.int32, y.shape, 1) < DIFF_HEAD_DIM
            s_lo = jnp.sum(jnp.where(lo, sq, 0.0), axis=-1, keepdims=True)
            s_hi = jnp.sum(jnp.where(lo, 0.0, sq), axis=-1, keepdims=True)
            inv = lax.rsqrt(jnp.where(lo, s_lo, s_hi) * (1.0 / DIFF_HEAD_DIM) + EPS)
            yn = y * inv * gain_ref[0, :, cols]
            o_ref[:, cols] = (yn * c_ref[...] + pltpu.roll(yn, LANES - half, 1) * sa_ref[...]
                              + pltpu.roll(yn, half, 1) * sb_ref[...])

    @pl.when(j >= n_qk_blocks)
    def _():
        o_ref[...] = y_all


def _diff_qkv(x, g, w_qkv, gains, c_tab, sa_tab, sb_tab):
    nt, d = x.shape
    cols = w_qkv.shape[1]
    tn = DIFF_QKV_BLOCK
    n_blocks = cols // tn
    n_qk_blocks = gains.shape[0]
    tm = _tile(nt, ROW_TILE)
    row = lambda i, j: (i, 0)
    return pl.pallas_call(
        functools.partial(_diff_qkv_kernel, n_qk_blocks=n_qk_blocks),
        out_shape=jax.ShapeDtypeStruct((nt, cols), F32),
        grid=(nt // tm, n_blocks),
        in_specs=[
            pl.BlockSpec((tm, d), row),
            pl.BlockSpec((1, d), lambda i, j: (0, 0)),
            pl.BlockSpec((d, tn), lambda i, j: (0, j)),
            pl.BlockSpec((1, 1, tn), lambda i, j: (jnp.minimum(j, n_qk_blocks - 1), 0, 0)),
            pl.BlockSpec((tm, LANES), row),
            pl.BlockSpec((tm, LANES), row),
            pl.BlockSpec((tm, LANES), row),
        ],
        out_specs=pl.BlockSpec((tm, tn), lambda i, j: (i, j)),
        scratch_shapes=[pltpu.VMEM((tm, d), BF16)],
        compiler_params=_params(("parallel", "arbitrary")),
        name="diff_qkv",
    )(x, g.reshape(1, d), w_qkv, gains, c_tab, sa_tab, sb_tab)


def _diff_lambda(lq_ref, lk_ref, lam_init):
    e = jnp.exp(jnp.sum(lq_ref[...] * lk_ref[...], axis=-1, keepdims=True))
    return e[0:1] - e[1:2] + lam_init


def _diff_finish(a0, l0, a1, l1, lam, og, lam_init):
    o = a0 / l0 - lam * (a1 / l1)
    return _rms(o, og) * (1.0 - lam_init)


def _component_masks(x):
    lo = lax.broadcasted_iota(jnp.int32, x.shape, 1) < DIFF_HEAD_DIM
    return jnp.where(lo, x, 0.0).astype(BF16), jnp.where(lo, 0.0, x).astype(BF16)


def _diff_flash_kernel(q_ref, k_ref, v_ref, lq_ref, lk_ref, og_ref, o_ref, qs_scr, m_scr, l_scr, acc_scr, *, lam_init):
    qi, ki = pl.program_id(2), pl.program_id(3)
    tq, tk = q_ref.shape[0], k_ref.shape[0]
    last = ((qi + 1) * tq - 1) // tk

    @pl.when(ki == 0)
    def _():
        m_scr[...] = jnp.full_like(m_scr, -jnp.inf)
        l_scr[...] = jnp.zeros_like(l_scr)
        acc_scr[...] = jnp.zeros_like(acc_scr)
        for g in range(DIFF_GROUP):
            q0, q1 = _component_masks(q_ref[:, g * LANES:(g + 1) * LANES])
            qs_scr[0, g * tq:(g + 1) * tq, :] = q0
            qs_scr[1, g * tq:(g + 1) * tq, :] = q1

    def step(masked):
        kb = k_ref[...].astype(BF16)
        vb = v_ref[...].astype(BF16)
        for c in range(2):
            s = _dot_nt(qs_scr[c], kb)
            if masked:
                rel = lax.broadcasted_iota(jnp.int32, s.shape, 0) % tq - lax.broadcasted_iota(jnp.int32, s.shape, 1)
                s = jnp.where(rel >= ki * tk - qi * tq, s, -jnp.inf)
            _softmax_step(s, vb, m_scr.at[c], l_scr.at[c], acc_scr.at[c])

    visible = (ki + 1) * tk <= qi * tq + 1

    @pl.when(visible)
    def _():
        step(False)

    @pl.when(jnp.logical_not(visible) & (ki <= last))
    def _():
        step(True)

    @pl.when(ki == last)
    def _():
        lam = _diff_lambda(lq_ref, lk_ref, lam_init)
        for g in range(DIFF_GROUP):
            r = slice(g * tq, (g + 1) * tq)
            o = _diff_finish(acc_scr[0, r], l_scr[0, r], acc_scr[1, r], l_scr[1, r], lam, og_ref[...], lam_init)
            o_ref[:, g * LANES:(g + 1) * LANES] = o.astype(BF16)


def _diff_flash(qkv, lam_q, lam_k, o_gain, lam_init, batch, seq):
    tq, tk = _tile(seq, DIFF_FLASH_Q_TILE), _tile(seq, DIFF_FLASH_K_TILE)
    nq, nk = seq // tq, seq // tk
    gw = DIFF_GROUP * LANES
    kcol0 = DIFF_HEADS
    vcol0 = DIFF_HEADS + DIFF_KV_HEADS
    fixed = lambda b, n, i, j: (0, 0)

    def kv_rows(b, i, j):
        return b * nk + jnp.minimum(j, ((i + 1) * tq - 1) // tk)

    return pl.pallas_call(
        functools.partial(_diff_flash_kernel, lam_init=lam_init),
        out_shape=jax.ShapeDtypeStruct((batch * seq, DIFF_HEADS * LANES), BF16),
        grid=(batch, DIFF_KV_HEADS, nq, nk),
        in_specs=[
            pl.BlockSpec((tq, gw), lambda b, n, i, j: (b * nq + i, n)),
            pl.BlockSpec((tk, LANES), lambda b, n, i, j: (kv_rows(b, i, j), kcol0 + n)),
            pl.BlockSpec((tk, LANES), lambda b, n, i, j: (kv_rows(b, i, j), vcol0 + n)),
            pl.BlockSpec((2, DIFF_HEAD_DIM), fixed),
            pl.BlockSpec((2, DIFF_HEAD_DIM), fixed),
            pl.BlockSpec((1, LANES), fixed),
        ],
        out_specs=pl.BlockSpec((tq, gw), lambda b, n, i, j: (b * nq + i, n)),
        scratch_shapes=[
            pltpu.VMEM((2, DIFF_GROUP * tq, LANES), BF16),
            pltpu.VMEM((2, DIFF_GROUP * tq, 1), F32),
            pltpu.VMEM((2, DIFF_GROUP * tq, 1), F32),
            pltpu.VMEM((2, DIFF_GROUP * tq, LANES), F32),
        ],
        compiler_params=_params(("parallel", "parallel", "parallel", "arbitrary")),
        name="diff_flash",
    )(qkv, qkv, qkv, lam_q, lam_k, o_gain.reshape(1, LANES))


def _diff_decode_kernel(pt_ref, *refs, pps, dec_seq, lam_init):
    k_refs = refs[:pps]
    v_refs = refs[pps:2 * pps]
    new_ref, lq_ref, lk_ref, og_ref, o_ref, q_scr, kb_scr, vb_scr, m_scr, l_scr, acc_scr = refs[2 * pps:]
    p = pl.program_id(1)
    hrows = 2 * DIFF_GROUP * dec_seq
    qcols = DIFF_HEADS * LANES
    kvw = DIFF_KV_HEADS * LANES
    page = k_refs[0].shape[2] // DIFF_KV_HEADS

    @pl.when(p == 0)
    def _():
        for n in range(DIFF_KV_HEADS):
            parts = [_component_masks(new_ref[:, (n * DIFF_GROUP + g) * LANES:(n * DIFF_GROUP + g + 1) * LANES])
                     for g in range(DIFF_GROUP)]
            for c in range(2):
                for g in range(DIFF_GROUP):
                    r0 = n * hrows + (c * DIFF_GROUP + g) * dec_seq
                    q_scr[r0:r0 + dec_seq, :] = parts[g][c].astype(F32)
        m_scr[...] = jnp.full_like(m_scr, -jnp.inf)
        l_scr[...] = jnp.zeros_like(l_scr)
        acc_scr[...] = jnp.zeros_like(acc_scr)

    def attend(k_of, v_of, masked):
        s = jnp.concatenate([_dot_nt(q_scr[n * hrows:(n + 1) * hrows, :].astype(BF16), k_of(n))
                             for n in range(DIFF_KV_HEADS)], axis=0)
        if masked:
            qpos = lax.broadcasted_iota(jnp.int32, s.shape, 0) % dec_seq
            s = jnp.where(lax.broadcasted_iota(jnp.int32, s.shape, 1) <= qpos, s, -jnp.inf)
        m_old = m_scr[...]
        m_new = jnp.maximum(m_old, jnp.max(s, axis=-1, keepdims=True))
        alpha = jnp.exp2(m_old - m_new)
        pr = jnp.exp2(s - m_new)
        l_scr[...] = alpha * l_scr[...] + jnp.sum(pr, axis=-1, keepdims=True)
        pv = jnp.concatenate([_dot(pr[n * hrows:(n + 1) * hrows].astype(BF16), v_of(n))
                              for n in range(DIFF_KV_HEADS)], axis=0)
        acc_scr[...] = alpha * acc_scr[...] + pv
        m_scr[...] = m_new

    for j in range(pps):
        for n in range(DIFF_KV_HEADS):
            rows_n = pl.ds(n, page, stride=DIFF_KV_HEADS)
            kb_scr[n, j * page:(j + 1) * page, :] = k_refs[j][0, 0, rows_n, :].astype(BF16)
            vb_scr[n, j * page:(j + 1) * page, :] = v_refs[j][0, 0, rows_n, :].astype(BF16)
    attend(lambda n: kb_scr[n], lambda n: vb_scr[n], False)

    @pl.when(p == pl.num_programs(1) - 1)
    def _():
        zeros = jnp.zeros((LANES - dec_seq, LANES), F32)

        def new_of(col0):
            return lambda n: jnp.concatenate(
                [new_ref[:, col0 + n * LANES:col0 + (n + 1) * LANES], zeros], axis=0).astype(BF16)

        attend(new_of(qcols), new_of(qcols + kvw), True)
        lam = _diff_lambda(lq_ref, lk_ref, lam_init)
        half = DIFF_GROUP * dec_seq
        for n in range(DIFF_KV_HEADS):
            r0, r1 = slice(n * hrows, n * hrows + half), slice(n * hrows + half, (n + 1) * hrows)
            o = _diff_finish(acc_scr[r0], l_scr[r0], acc_scr[r1], l_scr[r1], lam, og_ref[...], lam_init)
            for g in range(DIFF_GROUP):
                col = (n * DIFF_GROUP + g) * LANES
                o_ref[:, col:col + LANES] = o[g * dec_seq:(g + 1) * dec_seq]


def _diff_decode(page_table, layer, k_cache, v_cache, qkv, lam_q, lam_k, o_gain, lam_init, n_p, dec_seq):
    dec_batch, n_pages = page_table.shape
    prow = k_cache.shape[2]
    assert dec_seq == SUBLANES and n_p % dec_seq == 0
    pps = _tile(n_pages, PAGES_PER_STEP, 1)
    rows = DIFF_KV_HEADS * 2 * DIFF_GROUP * dec_seq
    step_tokens = pps * prow // DIFF_KV_HEADS
    off = n_p // dec_seq
    cols = qkv.shape[1]

    def page_spec(j):
        return pl.BlockSpec((1, 1, prow, LANES), lambda n, p, pt: (layer, pt[n * n_pages + p * pps + j], 0, 0))

    fixed2 = lambda n, p, pt: (0, 0)
    grid_spec = pltpu.PrefetchScalarGridSpec(
        num_scalar_prefetch=1,
        grid=(dec_batch, n_pages // pps),
        in_specs=[page_spec(j) for j in range(pps)] * 2 + [
            pl.BlockSpec((dec_seq, cols), lambda n, p, pt: (off + n, 0)),
            pl.BlockSpec((2, DIFF_HEAD_DIM), fixed2),
            pl.BlockSpec((2, DIFF_HEAD_DIM), fixed2),
            pl.BlockSpec((1, LANES), fixed2),
        ],
        out_specs=pl.BlockSpec((dec_seq, DIFF_HEADS * LANES), lambda n, p, pt: (n, 0)),
        scratch_shapes=[
            pltpu.VMEM((rows, LANES), F32),
            pltpu.VMEM((DIFF_KV_HEADS, step_tokens, LANES), BF16),
            pltpu.VMEM((DIFF_KV_HEADS, step_tokens, LANES), BF16),
            pltpu.VMEM((rows, 1), F32),
            pltpu.VMEM((rows, 1), F32),
            pltpu.VMEM((rows, LANES), F32),
        ],
    )
    return pl.pallas_call(
        functools.partial(_diff_decode_kernel, pps=pps, dec_seq=dec_seq, lam_init=lam_init),
        out_shape=jax.ShapeDtypeStruct((dec_batch * dec_seq, DIFF_HEADS * LANES), F32),
        grid_spec=grid_spec,
        compiler_params=_params(("arbitrary", "arbitrary")),
        name="diff_decode",
    )(page_table.reshape(-1), *([k_cache] * pps), *([v_cache] * pps), qkv, lam_q, lam_k, o_gain.reshape(1, LANES))


def _angles(pos, dim, theta):
    inv = jnp.power(jnp.float32(theta), -jnp.arange(0, dim, 2, dtype=F32) / dim)
    ang = pos.astype(F32)[:, None] * inv[None, :]
    return jnp.cos(ang), jnp.sin(ang)


def _mla_rope_tables(pos):
    cos, sin = _angles(pos, MLA_ROPE, MLA_THETA)
    return jnp.concatenate([cos, cos], axis=1), jnp.concatenate([-sin, sin], axis=1)


def _diff_rope_tables(pos):
    cos, sin = _angles(pos, DIFF_ROT, DIFF_THETA)
    n, half = pos.shape[0], DIFF_ROT // 2
    rest = DIFF_HEAD_DIM - DIFF_ROT
    c = jnp.concatenate([cos, cos, jnp.ones((n, rest), F32)], axis=1)
    sa = jnp.concatenate([-sin, jnp.zeros((n, half + rest), F32)], axis=1)
    sb = jnp.concatenate([jnp.zeros((n, half), F32), sin, jnp.zeros((n, rest), F32)], axis=1)
    return tuple(jnp.concatenate([t, t], axis=1) for t in (c, sa, sb))


def kernel(x_prompt, x_sample, state_pool, cache_mla_latent, cache_mla_krope, cache_diff_k, cache_diff_v, page_table,
           norm_mix, norm_mlp, pool_w, pool_scale, mla_w_dq, mla_q_norm, mla_w_uq, mla_w_dkv, mla_kv_norm, mla_w_ukv,
           mla_q_gain, mla_k_gain, mla_w_o, diff_w_qkv, diff_q_gain, diff_k_gain, diff_lam_q, diff_lam_k, diff_o_gain,
           diff_w_o, mlp_w_in, mlp_w_out):
    b, s, d = x_prompt.shape
    db, ds, _ = x_sample.shape
    depth = norm_mix.shape[0]
    page = cache_mla_latent.shape[2]
    past_len = page_table.shape[1] * page
    n_p, n_s = b * s, db * ds
    kvl = mla_kv_norm.shape[1]

    x = jnp.concatenate([x_prompt.reshape(n_p, d), x_sample.reshape(n_s, d)], axis=0)
    pos = jnp.concatenate([jnp.tile(jnp.arange(s), b), jnp.tile(past_len + jnp.arange(ds), db)])
    mla_cos, mla_sin = _mla_rope_tables(pos)
    diff_tabs = _diff_rope_tables(pos)
    w_in_b, w_out_b = mlp_w_in.astype(BF16), mlp_w_out.astype(BF16)

    pool_p, pool_s, lat_p, lat_s, kr_p, kr_s, dk_p, dk_s, dv_p, dv_s = ([] for _ in range(10))
    for i in range(depth):
        kind, j = i % N_MIXERS, i // N_MIXERS
        if kind == 0:
            x, st_p, st_s = _pool_layer(x, state_pool[j], norm_mix[i], pool_w[j].astype(BF16), pool_scale[j],
                                        batch=b, seq=s, dec_batch=db, dec_seq=ds, past_len=past_len)
            pool_p.append(st_p)
            pool_s.append(st_s)
        elif kind == 1:
            cq, lat, kr = _mla_down(x, norm_mix[i], mla_w_dq[j].astype(BF16), mla_w_dkv[j].astype(BF16),
                                    mla_q_norm[j], mla_kv_norm[j], mla_cos, mla_sin)
            ql = cq.shape[1]
            w_uq_h = mla_w_uq[j].astype(BF16).reshape(ql, MLA_HEADS, MLA_QK).transpose(1, 0, 2)
            w_ukv = mla_w_ukv[j].astype(BF16).reshape(kvl, MLA_HEADS, MLA_NOPE + MLA_V)
            w_ukv_h = w_ukv.transpose(1, 0, 2)
            wk_t_h = w_ukv[:, :, :MLA_NOPE].transpose(1, 2, 0)
            q = _mla_q(cq, w_uq_h, mla_cos, mla_sin, mla_q_gain[j])
            k, v = _mla_kv(lat, kr, w_ukv_h, mla_k_gain[j], n_p)
            o_p = _mla_flash(q, k, v, b, s)
            qabs = _mla_absorb(q, wk_t_h, mla_k_gain[j], n_p, n_s)
            olat = _mla_decode(page_table, j, cache_mla_latent, jnp.swapaxes(cache_mla_krope, 2, 3),
                               wk_t_h.reshape(MLA_HEADS * MLA_NOPE, kvl), qabs, q, lat, kr, mla_k_gain[j], n_p, ds)
            o_s = _mla_vexpand(olat, w_ukv_h[:, :, MLA_NOPE:], ds)
            x = _proj_residual(x, jnp.concatenate([o_p, o_s], axis=0), mla_w_o[j].astype(BF16))
            lat_p.append(lat[:n_p].reshape(b, s, kvl))
            lat_s.append(lat[n_p:].reshape(db, ds, kvl))
            kr_p.append(kr[:n_p].reshape(b, s, MLA_ROPE))
            kr_s.append(kr[n_p:].reshape(db, ds, MLA_ROPE))
        else:
            lam_init = 0.8 - 0.6 * math.exp(-0.3 * i)
            qg = jnp.tile((diff_q_gain[j] * (DIFF_HEAD_DIM ** -0.5 * LOG2E)).reshape(1, LANES), (1, DIFF_HEADS))
            kg = jnp.tile(diff_k_gain[j].reshape(1, LANES), (1, DIFF_KV_HEADS))
            gains = jnp.concatenate([qg, kg], axis=1).reshape(-1, 1, DIFF_QKV_BLOCK)
            qkv = _diff_qkv(x, norm_mix[i], diff_w_qkv[j].astype(BF16), gains, *diff_tabs)
            kvw = DIFF_KV_HEADS * LANES
            o_p = _diff_flash(qkv, diff_lam_q[j], diff_lam_k[j], diff_o_gain[j], lam_init, b, s)
            cshape = cache_diff_k.shape[:2] + (page * DIFF_KV_HEADS, LANES)
            o_s = _diff_decode(page_table, j, cache_diff_k.reshape(cshape), cache_diff_v.reshape(cshape),
                               qkv, diff_lam_q[j], diff_lam_k[j], diff_o_gain[j], lam_init, n_p, ds)
            x = _proj_residual(x, jnp.concatenate([o_p, o_s.astype(BF16)], axis=0), diff_w_o[j].astype(BF16))
            qc = DIFF_HEADS * LANES
            kk, vv = qkv[:, qc:qc + kvw], qkv[:, qc + kvw:]
            dk_p.append(kk[:n_p].reshape(b, s, DIFF_KV_HEADS, LANES))
            dk_s.append(kk[n_p:].reshape(db, ds, DIFF_KV_HEADS, LANES))
            dv_p.append(vv[:n_p].reshape(b, s, DIFF_KV_HEADS, LANES))
            dv_s.append(vv[n_p:].reshape(db, ds, DIFF_KV_HEADS, LANES))
        x = _mlp(x, norm_mlp[i], w_in_b, w_out_b, i)

    return (x[:n_p].reshape(b, s, d), x[n_p:].reshape(db, ds, d), jnp.stack(pool_p), jnp.stack(pool_s),
            jnp.stack(lat_p), jnp.stack(lat_s), jnp.stack(kr_p), jnp.stack(kr_s), jnp.stack(dk_p), jnp.stack(dk_s),
            jnp.stack(dv_p), jnp.stack(dv_s))
```

```python
import functools
import math

import jax
import jax.numpy as jnp
from jax import lax
from jax.experimental import pallas as pl
from jax.experimental.pallas import tpu as pltpu

F32 = jnp.float32
BF16 = jnp.bfloat16

EPS = 1e-6
POOL_WINDOWS = (2, 4, 8, 16)
POOL_STATE = 15
POOL_HALO = 16
MLA_HEADS = 16
MLA_NOPE = 128
MLA_ROPE = 64
MLA_V = 128
MLA_QK = MLA_NOPE + MLA_ROPE
MLA_THETA = 10000.0
DIFF_HEADS = 16
DIFF_KV_HEADS = 4
DIFF_GROUP = DIFF_HEADS // DIFF_KV_HEADS
DIFF_HEAD_DIM = 64
DIFF_ROT = DIFF_HEAD_DIM // 4
DIFF_THETA = 500000.0
N_MIXERS = 3

LANES = 128
SUBLANES = 8
VMEM_LIMIT_BYTES = 56 * 1024 * 1024

ROW_TILE = 512
MLP_ROW_TILE = 768
POOL_TILE = 256
FFN_TILE = 1024
MLA_FLASH_Q_TILE = 512
MLA_FLASH_K_TILE = 2048
DIFF_FLASH_Q_TILE = 256
DIFF_FLASH_K_TILE = 1024
PAGES_PER_STEP = 16
DIFF_PAGES_PER_STEP = 16
CHUNK_PAGES = 2
HEADS_PER_STEP = 4
FLASH_HEADS_PER_STEP = 2
SOFTMAX_ROWS = 16
DIFF_QKV_BLOCK = DIFF_KV_HEADS * LANES

LOG2E = math.log2(math.e)


def _tile(n, pref, mult=16):
    if n <= pref:
        return n
    for t in range(pref, 0, -1):
        if n % t == 0 and t % mult == 0:
            return t
    raise ValueError(f"no tile for {n} (pref {pref}, mult {mult})")


def _params(sem):
    return pltpu.CompilerParams(dimension_semantics=sem, vmem_limit_bytes=VMEM_LIMIT_BYTES)


def _rms(x, g, n=None):
    ms = jnp.sum(x * x, axis=-1, keepdims=True) * (1.0 / (n or x.shape[-1]))
    return x * lax.rsqrt(ms + EPS) * g


def _dot(a, b):
    return jnp.dot(a, b, preferred_element_type=F32)


def _dot_nt(a, b):
    return lax.dot_general(a, b, (((1,), (1,)), ((), ())), preferred_element_type=F32)


def _swap_halves(r):
    half = r.shape[-1] // 2
    return jnp.concatenate([r[:, half:], r[:, :half]], axis=-1)


def _mlp_kernel(x_ref, g_ref, win_ref, wout_ref, o_ref, h_scr, acc_scr):
    f = pl.program_id(1)

    @pl.when(f == 0)
    def _():
        h_scr[...] = _rms(x_ref[...], g_ref[...]).astype(BF16)
        acc_scr[...] = jnp.zeros_like(acc_scr)

    a = jnp.maximum(_dot(h_scr[...], win_ref[0]), 0.0)
    acc_scr[...] += _dot((a * a).astype(BF16), wout_ref[0])

    @pl.when(f == pl.num_programs(1) - 1)
    def _():
        o_ref[...] = x_ref[...] + acc_scr[...]


def _mlp(x, g, w_in, w_out, layer):
    nt, d = x.shape
    ffn = w_in.shape[2]
    tm, tf = _tile(nt, MLP_ROW_TILE), _tile(ffn, FFN_TILE, LANES)
    return pl.pallas_call(
        _mlp_kernel,
        out_shape=jax.ShapeDtypeStruct((nt, d), F32),
        grid=(nt // tm, ffn // tf),
        in_specs=[
            pl.BlockSpec((tm, d), lambda i, f: (i, 0)),
            pl.BlockSpec((1, d), lambda i, f: (0, 0)),
            pl.BlockSpec((1, d, tf), lambda i, f: (layer, 0, f)),
            pl.BlockSpec((1, tf, d), lambda i, f: (layer, f, 0)),
        ],
        out_specs=pl.BlockSpec((tm, d), lambda i, f: (i, 0)),
        scratch_shapes=[pltpu.VMEM((tm, d), BF16), pltpu.VMEM((tm, d), F32)],
        compiler_params=_params(("parallel", "arbitrary")),
        name="mlp",
    )(x, g.reshape(1, d), w_in, w_out)


def _proj_kernel(x_ref, a_ref, w_ref, o_ref):
    o_ref[...] = x_ref[...] + _dot(a_ref[...], w_ref[...])


def _proj_residual(x, a, w):
    nt, d = x.shape
    k = a.shape[1]
    tm = _tile(nt, ROW_TILE)
    return pl.pallas_call(
        _proj_kernel,
        out_shape=jax.ShapeDtypeStruct((nt, d), F32),
        grid=(nt // tm,),
        in_specs=[
            pl.BlockSpec((tm, d), lambda i: (i, 0)),
            pl.BlockSpec((tm, k), lambda i: (i, 0)),
            pl.BlockSpec((k, d), lambda i: (0, 0)),
        ],
        out_specs=pl.BlockSpec((tm, d), lambda i: (i, 0)),
        compiler_params=_params(("parallel",)),
        name="proj_residual",
    )(x, a, w)


def _pool_kernel(x_ref, halo_ref, st_ref, g_ref, w_ref, sc_ref, o_ref, tail_ref, sto_ref, ext_scr, exts_scr,
                 *, n_ptiles, tiles_per_seq, dec_seq, past_len):
    i = pl.program_id(0)
    t = x_ref.shape[0]
    gdim = w_ref.shape[1]
    g = g_ref[...]
    xm = x_ref[...]
    hm = _rms(xm, g)
    row = lax.broadcasted_iota(jnp.int32, (t, 1), 0)

    def mix(gi, wsum, cnt):
        c0 = gi * gdim
        delta = wsum / cnt - hm[:, c0:c0 + gdim]
        y = _dot(delta.astype(BF16), w_ref[gi]) * sc_ref[:, c0:c0 + gdim]
        o_ref[:, c0:c0 + gdim] = xm[:, c0:c0 + gdim] + y

    @pl.when(i < n_ptiles)
    def _():
        first = (i % tiles_per_seq) == 0
        hh = _rms(halo_ref[...], g)
        ext_scr[0:POOL_HALO, :] = jnp.where(first, 0.0, hh)
        ext_scr[POOL_HALO:POOL_HALO + t, :] = hm
        tail_ref[0] = hm[t - POOL_HALO:t, :]
        pos = (i % tiles_per_seq) * t + row
        for gi, w in enumerate(POOL_WINDOWS):
            c0 = gi * gdim
            wsum = hm[:, c0:c0 + gdim]
            for k in range(1, w):
                wsum = wsum + ext_scr[POOL_HALO - k:POOL_HALO - k + t, c0:c0 + gdim]
            mix(gi, wsum, jnp.minimum(pos + 1, w).astype(F32))

    @pl.when(i >= n_ptiles)
    def _():
        nseq = t // dec_seq
        hm3 = hm.reshape(nseq, dec_seq, hm.shape[1])
        exts_scr[:, 0:POOL_HALO, :] = st_ref[...]
        exts_scr[:, POOL_HALO:POOL_HALO + dec_seq, :] = hm3
        sto_ref[...] = exts_scr[:, dec_seq:dec_seq + POOL_HALO, :]
        pos = past_len + row % dec_seq
        for gi, w in enumerate(POOL_WINDOWS):
            c0 = gi * gdim
            wsum = hm3[:, :, c0:c0 + gdim]
            for k in range(1, w):
                wsum = wsum + exts_scr[:, POOL_HALO - k:POOL_HALO - k + dec_seq, c0:c0 + gdim]
            mix(gi, wsum.reshape(t, gdim), jnp.minimum(pos + 1, w).astype(F32))


def _pool_layer(x, state, g, w_grp, scale, *, batch, seq, dec_batch, dec_seq, past_len):
    nt, d = x.shape
    assert dec_seq == SUBLANES and POOL_HALO == 2 * dec_seq
    n_p, n_s = batch * seq, dec_batch * dec_seq
    t = _tile(math.gcd(seq, n_s), POOL_TILE)
    n_ptiles, n_stiles, tps = n_p // t, n_s // t, seq // t
    nseq = t // dec_seq
    state16 = jnp.pad(state, ((0, 0), (POOL_HALO - POOL_STATE, 0), (0, 0)))
    hb = t // POOL_HALO
    kern = functools.partial(_pool_kernel, n_ptiles=n_ptiles, tiles_per_seq=tps, dec_seq=dec_seq, past_len=past_len)
    xo, tail, sto = pl.pallas_call(
        kern,
        out_shape=(
            jax.ShapeDtypeStruct((nt, d), F32),
            jax.ShapeDtypeStruct((batch, POOL_HALO, d), F32),
            jax.ShapeDtypeStruct((dec_batch, POOL_HALO, d), F32),
        ),
        grid=(n_ptiles + n_stiles,),
        in_specs=[
            pl.BlockSpec((t, d), lambda i: (i, 0)),
            pl.BlockSpec((POOL_HALO, d), lambda i: (jnp.maximum(i * hb - 1, 0), 0)),
            pl.BlockSpec((nseq, POOL_HALO, d), lambda i: (jnp.maximum(i - n_ptiles, 0), 0, 0)),
            pl.BlockSpec((1, d), lambda i: (0, 0)),
            pl.BlockSpec(w_grp.shape, lambda i: (0, 0, 0)),
            pl.BlockSpec((1, d), lambda i: (0, 0)),
        ],
        out_specs=(
            pl.BlockSpec((t, d), lambda i: (i, 0)),
            pl.BlockSpec((1, POOL_HALO, d), lambda i: (jnp.minimum(i // tps, batch - 1), 0, 0)),
            pl.BlockSpec((nseq, POOL_HALO, d), lambda i: (jnp.maximum(i - n_ptiles, 0), 0, 0)),
        ),
        scratch_shapes=[pltpu.VMEM((t + POOL_HALO, d), F32), pltpu.VMEM((nseq, POOL_HALO + dec_seq, d), F32)],
        compiler_params=_params(("arbitrary",)),
        name="pool_mixer",
    )(x, x, state16, g.reshape(1, d), w_grp, scale.reshape(1, d))
    return xo, tail[:, 1:], sto[:, 1:]


def _mla_down_kernel(x_ref, g_ref, wdq_ref, wdkv_ref, qn_ref, kvn_ref, cos_ref, sin_ref, cq_ref, lat_ref, kr_ref):
    h = _rms(x_ref[...], g_ref[...]).astype(BF16)
    cq_ref[...] = _rms(_dot(h, wdq_ref[...]), qn_ref[...])
    ckr = _dot(h, wdkv_ref[...])
    lora = lat_ref.shape[1]
    lat_ref[...] = _rms(ckr[:, :lora], kvn_ref[...])
    r = ckr[:, lora:]
    kr_ref[...] = r * cos_ref[...] + _swap_halves(r) * sin_ref[...]


def _mla_down(x, g, w_dq, w_dkv, q_norm, kv_norm, cos, sin):
    nt, d = x.shape
    ql, kvl = w_dq.shape[1], kv_norm.shape[0]
    tm = _tile(nt, ROW_TILE)
    row = lambda i: (i, 0)
    fixed = lambda i: (0, 0)
    return pl.pallas_call(
        _mla_down_kernel,
        out_shape=(
            jax.ShapeDtypeStruct((nt, ql), F32),
            jax.ShapeDtypeStruct((nt, kvl), F32),
            jax.ShapeDtypeStruct((nt, MLA_ROPE), F32),
        ),
        grid=(nt // tm,),
        in_specs=[
            pl.BlockSpec((tm, d), row),
            pl.BlockSpec((1, d), fixed),
            pl.BlockSpec(w_dq.shape, fixed),
            pl.BlockSpec(w_dkv.shape, fixed),
            pl.BlockSpec((1, ql), fixed),
            pl.BlockSpec((1, kvl), fixed),
            pl.BlockSpec((tm, MLA_ROPE), row),
            pl.BlockSpec((tm, MLA_ROPE), row),
        ],
        out_specs=(pl.BlockSpec((tm, ql), row), pl.BlockSpec((tm, kvl), row), pl.BlockSpec((tm, MLA_ROPE), row)),
        compiler_params=_params(("parallel",)),
        name="mla_down",
    )(x, g.reshape(1, d), w_dq, w_dkv, q_norm.reshape(1, ql), kv_norm.reshape(1, kvl), cos, sin)


def _mla_q_kernel(cq_ref, w_ref, cos_ref, sin_ref, g_ref, q_ref):
    cq = cq_ref[...].astype(BF16)
    for h in range(w_ref.shape[0]):
        q = _dot(cq, w_ref[h])
        qn = q[:, :MLA_NOPE]
        r = q[:, MLA_NOPE:]
        rr = r * cos_ref[...] + _swap_halves(r) * sin_ref[...]
        ss = jnp.sum(qn * qn, axis=-1, keepdims=True) + jnp.sum(rr * rr, axis=-1, keepdims=True)
        inv = lax.rsqrt(ss * (1.0 / MLA_QK) + EPS) * (MLA_QK ** -0.5 * LOG2E)
        q_ref[h, :, :MLA_NOPE] = qn * inv * g_ref[:, :MLA_NOPE]
        q_ref[h, :, MLA_NOPE:] = rr * inv * g_ref[:, MLA_NOPE:]


def _mla_q(cq, w_uq_h, cos, sin, q_gain):
    nt, ql = cq.shape
    tm = _tile(nt, ROW_TILE)
    hps = HEADS_PER_STEP
    return pl.pallas_call(
        _mla_q_kernel,
        out_shape=jax.ShapeDtypeStruct((MLA_HEADS, nt, MLA_QK), F32),
        grid=(nt // tm, MLA_HEADS // hps),
        in_specs=[
            pl.BlockSpec((tm, ql), lambda i, h: (i, 0)),
            pl.BlockSpec((hps, ql, MLA_QK), lambda i, h: (h, 0, 0)),
            pl.BlockSpec((tm, MLA_ROPE), lambda i, h: (i, 0)),
            pl.BlockSpec((tm, MLA_ROPE), lambda i, h: (i, 0)),
            pl.BlockSpec((1, MLA_QK), lambda i, h: (0, 0)),
        ],
        out_specs=pl.BlockSpec((hps, tm, MLA_QK), lambda i, h: (h, i, 0)),
        compiler_params=_params(("parallel", "arbitrary")),
        name="mla_q",
    )(cq, w_uq_h, cos, sin, q_gain.reshape(1, MLA_QK))


def _mla_kv_kernel(lat_ref, kr_ref, w_ref, g_ref, k_ref, v_ref):
    latb = lat_ref[...].astype(BF16)
    kr = kr_ref[...]
    sskr = jnp.sum(kr * kr, axis=-1, keepdims=True)
    for h in range(w_ref.shape[0]):
        kv = _dot(latb, w_ref[h])
        kn = kv[:, :MLA_NOPE]
        inv = lax.rsqrt((jnp.sum(kn * kn, axis=-1, keepdims=True) + sskr) * (1.0 / MLA_QK) + EPS)
        k_ref[h, :, :MLA_NOPE] = (kn * inv * g_ref[:, :MLA_NOPE]).astype(BF16)
        k_ref[h, :, MLA_NOPE:] = (kr * inv * g_ref[:, MLA_NOPE:]).astype(BF16)
        v_ref[h] = kv[:, MLA_NOPE:].astype(BF16)


def _mla_kv(lat, kr, w_ukv_h, k_gain, n_p):
    kvl = lat.shape[1]
    tm = _tile(n_p, ROW_TILE)
    hps = HEADS_PER_STEP
    return pl.pallas_call(
        _mla_kv_kernel,
        out_shape=(
            jax.ShapeDtypeStruct((MLA_HEADS, n_p, MLA_QK), BF16),
            jax.ShapeDtypeStruct((MLA_HEADS, n_p, MLA_V), BF16),
        ),
        grid=(n_p // tm, MLA_HEADS // hps),
        in_specs=[
            pl.BlockSpec((tm, kvl), lambda i, h: (i, 0)),
            pl.BlockSpec((tm, MLA_ROPE), lambda i, h: (i, 0)),
            pl.BlockSpec((hps, kvl, MLA_NOPE + MLA_V), lambda i, h: (h, 0, 0)),
            pl.BlockSpec((1, MLA_QK), lambda i, h: (0, 0)),
        ],
        out_specs=(
            pl.BlockSpec((hps, tm, MLA_QK), lambda i, h: (h, i, 0)),
            pl.BlockSpec((hps, tm, MLA_V), lambda i, h: (h, i, 0)),
        ),
        compiler_params=_params(("parallel", "arbitrary")),
        name="mla_kv",
    )(lat, kr, w_ukv_h, k_gain.reshape(1, MLA_QK))


def _softmax_step(s, v, m_scr, l_scr, acc_scr):
    m_old = m_scr[...]
    m_new = jnp.maximum(m_old, jnp.max(s, axis=-1, keepdims=True))
    alpha = jnp.exp2(m_old - m_new)
    p = jnp.exp2(s - m_new)
    l_scr[...] = alpha * l_scr[...] + jnp.sum(p, axis=-1, keepdims=True)
    acc_scr[...] = alpha * acc_scr[...] + _dot(p.astype(BF16), v)
    m_scr[...] = m_new


def _softmax_rows(s_scr, p_scr, al_scr, m_scr, l_scr, min_rel=None, row_period=None):
    rows, keys = s_scr.shape
    for r0 in range(0, rows, SOFTMAX_ROWS):
        rs = slice(r0, r0 + SOFTMAX_ROWS)
        s = s_scr[rs, :]
        if min_rel is not None:
            shape = (SOFTMAX_ROWS, keys)
            rel = lax.broadcasted_iota(jnp.int32, shape, 0) - lax.broadcasted_iota(jnp.int32, shape, 1)
            s = jnp.where(rel >= min_rel - r0 % (row_period or rows), s, -jnp.inf)
        m_old = m_scr[rs, :]
        m_new = jnp.maximum(m_old, jnp.max(s, axis=-1, keepdims=True))
        alpha = jnp.exp2(m_old - m_new)
        p = jnp.exp2(s - m_new)
        l_scr[rs, :] = alpha * l_scr[rs, :] + jnp.sum(p, axis=-1, keepdims=True)
        m_scr[rs, :] = m_new
        al_scr[rs, :] = alpha
        p_scr[rs, :] = p.astype(BF16)


def _attend_units(units, min_rel=None, row_period=None):
    for q, k, _, st in units:
        st[0][...] = _dot_nt(q, k)
    for _, _, v, (s_scr, p_scr, al_scr, m_scr, l_scr, acc_scr) in units:
        _softmax_rows(s_scr, p_scr, al_scr, m_scr, l_scr, min_rel, row_period)
        acc_scr[...] = al_scr[...] * acc_scr[...] + _dot(p_scr[...], v)


def _causal_mask(s, q0, k0):
    qi = q0 + lax.broadcasted_iota(jnp.int32, s.shape, 0)
    ki = k0 + lax.broadcasted_iota(jnp.int32, s.shape, 1)
    return jnp.where(qi >= ki, s, -jnp.inf)


def _mla_flash_kernel(q_ref, k_ref, v_ref, o_ref, s_scr, p_scr, al_scr, m_scr, l_scr, acc_scr):
    qi, ki = pl.program_id(2), pl.program_id(3)
    nh, tq, tk = q_ref.shape[0], q_ref.shape[1], k_ref.shape[1]
    last = ((qi + 1) * tq - 1) // tk
    visible = (ki + 1) * tk <= qi * tq + 1

    @pl.when(ki == 0)
    def _():
        m_scr[...] = jnp.full_like(m_scr, -jnp.inf)
        l_scr[...] = jnp.zeros_like(l_scr)
        acc_scr[...] = jnp.zeros_like(acc_scr)

    def units():
        return [(q_ref[h].astype(BF16), k_ref[h], v_ref[h],
                 tuple(r.at[h] for r in (s_scr, p_scr, al_scr, m_scr, l_scr, acc_scr))) for h in range(nh)]

    @pl.when(visible)
    def _():
        _attend_units(units())

    @pl.when(jnp.logical_not(visible) & (ki <= last))
    def _():
        _attend_units(units(), min_rel=ki * tk - qi * tq)

    @pl.when(ki == last)
    def _():
        for h in range(nh):
            o_ref[:, h * MLA_V:(h + 1) * MLA_V] = (acc_scr[h] / l_scr[h]).astype(BF16)


def _mla_flash(q, k, v, batch, seq):
    tq, tk = _tile(seq, MLA_FLASH_Q_TILE), _tile(seq, MLA_FLASH_K_TILE)
    nq, nk = seq // tq, seq // tk
    nh = FLASH_HEADS_PER_STEP

    def kv_rows(b, i, j):
        return b * nk + jnp.minimum(j, ((i + 1) * tq - 1) // tk)

    return pl.pallas_call(
        _mla_flash_kernel,
        out_shape=jax.ShapeDtypeStruct((batch * seq, MLA_HEADS * MLA_V), BF16),
        grid=(batch, MLA_HEADS // nh, nq, nk),
        in_specs=[
            pl.BlockSpec((nh, tq, MLA_QK), lambda b, h, i, j: (h, b * nq + i, 0)),
            pl.BlockSpec((nh, tk, MLA_QK), lambda b, h, i, j: (h, kv_rows(b, i, j), 0)),
            pl.BlockSpec((nh, tk, MLA_V), lambda b, h, i, j: (h, kv_rows(b, i, j), 0)),
        ],
        out_specs=pl.BlockSpec((tq, nh * MLA_V), lambda b, h, i, j: (b * nq + i, h)),
        scratch_shapes=[pltpu.VMEM((nh, tq, tk), F32), pltpu.VMEM((nh, tq, tk), BF16), pltpu.VMEM((nh, tq, 1), F32),
                        pltpu.VMEM((nh, tq, 1), F32), pltpu.VMEM((nh, tq, 1), F32), pltpu.VMEM((nh, tq, MLA_V), F32)],
        compiler_params=_params(("parallel", "parallel", "parallel", "arbitrary")),
        name="mla_flash",
    )(q, k, v)


def _mla_absorb_kernel(q_ref, w_ref, g_ref, o_ref):
    qn = (q_ref[0][:, :MLA_NOPE] * g_ref[:, :MLA_NOPE]).astype(BF16)
    o_ref[0] = _dot(qn, w_ref[0])


def _mla_absorb(q, wk_t_h, k_gain, n_p, n_s):
    kvl = wk_t_h.shape[2]
    ta = math.gcd(n_p, n_s)
    while ta > 1024:
        ta //= 2
    off = n_p // ta
    return pl.pallas_call(
        _mla_absorb_kernel,
        out_shape=jax.ShapeDtypeStruct((MLA_HEADS, n_s, kvl), F32),
        grid=(MLA_HEADS, n_s // ta),
        in_specs=[
            pl.BlockSpec((1, ta, MLA_QK), lambda h, i: (h, off + i, 0)),
            pl.BlockSpec((1, MLA_NOPE, kvl), lambda h, i: (h, 0, 0)),
            pl.BlockSpec((1, MLA_QK), lambda h, i: (0, 0)),
        ],
        out_specs=pl.BlockSpec((1, ta, kvl), lambda h, i: (h, i, 0)),
        compiler_params=_params(("parallel", "arbitrary")),
        name="mla_absorb_q",
    )(q, wk_t_h, k_gain.reshape(1, MLA_QK))


def _mla_decode_kernel(pt_ref, *refs, pps, dec_seq):
    lat_refs = refs[:pps]
    krt_refs = refs[pps:2 * pps]
    (wkt_ref, qabs_ref, q_ref, nlat_ref, nkr_ref, g_ref, o_ref,
     lhs_scr, qr_scr, latb_scr, big_scr, m_scr, l_scr, acc_scr) = refs[2 * pps:]
    n, p = pl.program_id(0), pl.program_id(1)
    nk = wkt_ref.shape[0]
    rows = MLA_HEADS * dec_seq
    page = lat_refs[0].shape[2]
    cn = big_scr.shape[2]
    cpages = cn // page

    @pl.when((n == 0) & (p == 0))
    def _():
        lhs_scr[0:nk, :] = wkt_ref[...]

    @pl.when(p == 0)
    def _():
        lhs_scr[nk:nk + rows, :] = qabs_ref[...].reshape(rows, qabs_ref.shape[2]).astype(BF16)
        qr = q_ref[...][:, :, MLA_NOPE:].reshape(rows, MLA_ROPE)
        qr_scr[...] = (qr * g_ref[:, MLA_NOPE:]).astype(BF16)
        m_scr[...] = jnp.full_like(m_scr, -jnp.inf)
        l_scr[...] = jnp.zeros_like(l_scr)
        acc_scr[...] = jnp.zeros_like(acc_scr)

    def scores(big, sskr, s_rope):
        nn = big.shape[1]
        kt = big[:nk]
        ssn = jnp.sum((kt * kt).reshape(MLA_HEADS, MLA_NOPE, nn), axis=1)
        inv = lax.rsqrt((ssn + sskr) * (1.0 / MLA_QK) + EPS)
        inv = jnp.broadcast_to(inv[:, None, :], (MLA_HEADS, dec_seq, nn)).reshape(rows, nn)
        return (big[nk:] + s_rope) * inv

    for j in range(pps):
        latb_scr[j * page:(j + 1) * page, :] = lat_refs[j][0, 0].astype(BF16)
    for c in range(pps // cpages):
        big_scr[c] = _dot_nt(lhs_scr[...], latb_scr[c * cn:(c + 1) * cn, :])
    parts = []
    for c in range(pps // cpages):
        krt = jnp.concatenate([krt_refs[c * cpages + j][0, 0] for j in range(cpages)], axis=1)
        sskr = jnp.sum(krt * krt, axis=0, keepdims=True)
        parts.append(scores(big_scr[c], sskr, _dot(qr_scr[...], krt.astype(BF16))))
    _softmax_step(jnp.concatenate(parts, axis=1), latb_scr[...], m_scr, l_scr, acc_scr)

    @pl.when(p == pl.num_programs(1) - 1)
    def _():
        pad = LANES - dec_seq
        latn = jnp.concatenate([nlat_ref[...], jnp.zeros((pad, nlat_ref.shape[1]), F32)], axis=0).astype(BF16)
        kr = jnp.concatenate([nkr_ref[...], jnp.zeros((pad, MLA_ROPE), F32)], axis=0)
        sq = kr * kr
        hi = sq.astype(BF16)
        lo = (sq - hi.astype(F32)).astype(BF16)
        ones = jnp.ones((SUBLANES, MLA_ROPE), BF16)
        sskr = (_dot_nt(ones, hi) + _dot_nt(ones, lo))[0:1]
        s = scores(_dot_nt(lhs_scr[...], latn), sskr, _dot_nt(qr_scr[...], kr.astype(BF16)))
        qpos = lax.broadcasted_iota(jnp.int32, s.shape, 0) % dec_seq
        s = jnp.where(lax.broadcasted_iota(jnp.int32, s.shape, 1) <= qpos, s, -jnp.inf)
        _softmax_step(s, latn, m_scr, l_scr, acc_scr)
        o_ref[0] = acc_scr[...] / l_scr[...]


def _mla_decode(page_table, layer, lat_cache, krt_cache, wk_t, qabs, q, lat, kr, k_gain, n_p, dec_seq):
    dec_batch, n_pages = page_table.shape
    page, kvl = lat_cache.shape[2], lat_cache.shape[3]
    assert dec_seq == SUBLANES and n_p % dec_seq == 0
    pps = _tile(n_pages, PAGES_PER_STEP, 1)
    cpages = _tile(pps, CHUNK_PAGES, 1)
    rows = MLA_HEADS * dec_seq
    off = n_p // dec_seq

    def page_spec(shape, j):
        return pl.BlockSpec((1, 1) + shape, lambda n, p, pt: (layer, pt[n * n_pages + p * pps + j], 0, 0))

    fixed2 = lambda n, p, pt: (0, 0)
    grid_spec = pltpu.PrefetchScalarGridSpec(
        num_scalar_prefetch=1,
        grid=(dec_batch, n_pages // pps),
        in_specs=[page_spec((page, kvl), j) for j in range(pps)] + [page_spec((MLA_ROPE, page), j) for j in range(pps)] + [
            pl.BlockSpec(wk_t.shape, fixed2),
            pl.BlockSpec((MLA_HEADS, dec_seq, kvl), lambda n, p, pt: (0, n, 0)),
            pl.BlockSpec((MLA_HEADS, dec_seq, MLA_QK), lambda n, p, pt: (0, off + n, 0)),
            pl.BlockSpec((dec_seq, kvl), lambda n, p, pt: (off + n, 0)),
            pl.BlockSpec((dec_seq, MLA_ROPE), lambda n, p, pt: (off + n, 0)),
            pl.BlockSpec((1, MLA_QK), fixed2),
        ],
        out_specs=pl.BlockSpec((1, rows, kvl), lambda n, p, pt: (n, 0, 0)),
        scratch_shapes=[
            pltpu.VMEM((wk_t.shape[0] + rows, kvl), BF16),
            pltpu.VMEM((rows, MLA_ROPE), BF16),
            pltpu.VMEM((pps * page, kvl), BF16),
            pltpu.VMEM((pps // cpages, wk_t.shape[0] + rows, cpages * page), F32),
            pltpu.VMEM((rows, 1), F32),
            pltpu.VMEM((rows, 1), F32),
            pltpu.VMEM((rows, kvl), F32),
        ],
    )
    return pl.pallas_call(
        functools.partial(_mla_decode_kernel, pps=pps, dec_seq=dec_seq),
        out_shape=jax.ShapeDtypeStruct((dec_batch, rows, kvl), F32),
        grid_spec=grid_spec,
        compiler_params=_params(("arbitrary", "arbitrary")),
        name="mla_decode",
    )(page_table.reshape(-1), *([lat_cache] * pps), *([krt_cache] * pps), wk_t, qabs, q, lat, kr,
      k_gain.reshape(1, MLA_QK))


def _mla_vexpand_kernel(a_ref, w_ref, o_ref):
    a = a_ref[...]
    o_ref[...] = _dot(a.reshape(a.shape[0] * a.shape[2], a.shape[3]).astype(BF16), w_ref[0]).astype(BF16)


def _mla_vexpand(olat, wv_h, dec_seq):
    dec_batch, _, kvl = olat.shape
    n_s = dec_batch * dec_seq
    return pl.pallas_call(
        _mla_vexpand_kernel,
        out_shape=jax.ShapeDtypeStruct((n_s, MLA_HEADS * MLA_V), BF16),
        grid=(MLA_HEADS,),
        in_specs=[
            pl.BlockSpec((dec_batch, 1, dec_seq, kvl), lambda h: (0, h, 0, 0)),
            pl.BlockSpec((1, kvl, MLA_V), lambda h: (h, 0, 0)),
        ],
        out_specs=pl.BlockSpec((n_s, MLA_V), lambda h: (0, h)),
        compiler_params=_params(("parallel",)),
        name="mla_vexpand",
    )(olat.reshape(dec_batch, MLA_HEADS, dec_seq, kvl), wv_h)


def _diff_qkv_kernel(x_ref, g_ref, w_ref, gain_ref, c_ref, sa_ref, sb_ref, o_ref, h_scr, *, n_qk_blocks):
    j = pl.program_id(1)

    @pl.when(j == 0)
    def _():
        h_scr[...] = _rms(x_ref[...], g_ref[...]).astype(BF16)

    y_all = _dot(h_scr[...], w_ref[...])

    @pl.when(j < n_qk_blocks)
    def _():
        half = DIFF_ROT // 2
        for b in range(y_all.shape[1] // LANES):
            cols = slice(b * LANES, (b + 1) * LANES)
            y = y_all[:, cols]
            sq = y * y
            lo = lax.broadcasted_iota(jnp.int32, y.shape, 1) < DIFF_HEAD_DIM
            s_lo = jnp.sum(jnp.where(lo, sq, 0.0), axis=-1, keepdims=True)
            s_hi = jnp.sum(jnp.where(lo, 0.0, sq), axis=-1, keepdims=True)
            inv = lax.rsqrt(jnp.where(lo, s_lo, s_hi) * (1.0 / DIFF_HEAD_DIM) + EPS)
            yn = y * inv * gain_ref[0, :, cols]
            o_ref[:, cols] = (yn * c_ref[...] + pltpu.roll(yn, LANES - half, 1) * sa_ref[...]
                              + pltpu.roll(yn, half, 1) * sb_ref[...])

    @pl.when(j >= n_qk_blocks)
    def _():
        o_ref[...] = y_all


def _diff_qkv(x, g, w_qkv, gains, c_tab, sa_tab, sb_tab):
    nt, d = x.shape
    cols = w_qkv.shape[1]
    tn = DIFF_QKV_BLOCK
    n_blocks = cols // tn
    n_qk_blocks = gains.shape[0]
    tm = _tile(nt, ROW_TILE)
    row = lambda i, j: (i, 0)
    return pl.pallas_call(
        functools.partial(_diff_qkv_kernel, n_qk_blocks=n_qk_blocks),
        out_shape=jax.ShapeDtypeStruct((nt, cols), F32),
        grid=(nt // tm, n_blocks),
        in_specs=[
            pl.BlockSpec((tm, d), row),
            pl.BlockSpec((1, d), lambda i, j: (0, 0)),
            pl.BlockSpec((d, tn), lambda i, j: (0, j)),
            pl.BlockSpec((1, 1, tn), lambda i, j: (jnp.minimum(j, n_qk_blocks - 1), 0, 0)),
            pl.BlockSpec((tm, LANES), row),
            pl.BlockSpec((tm, LANES), row),
            pl.BlockSpec((tm, LANES), row),
        ],
        out_specs=pl.BlockSpec((tm, tn), lambda i, j: (i, j)),
        scratch_shapes=[pltpu.VMEM((tm, d), BF16)],
        compiler_params=_params(("parallel", "arbitrary")),
        name="diff_qkv",
    )(x, g.reshape(1, d), w_qkv, gains, c_tab, sa_tab, sb_tab)


def _diff_lambda(lq_ref, lk_ref, lam_init):
    e = jnp.exp(jnp.sum(lq_ref[...] * lk_ref[...], axis=-1, keepdims=True))
    return e[0:1] - e[1:2] + lam_init


def _diff_finish(a0, l0, a1, l1, lam, og, lam_init):
    o = a0 / l0 - lam * (a1 / l1)
    return _rms(o, og) * (1.0 - lam_init)


def _component_masks(x):
    lo = lax.broadcasted_iota(jnp.int32, x.shape, 1) < DIFF_HEAD_DIM
    return jnp.where(lo, x, 0.0).astype(BF16), jnp.where(lo, 0.0, x).astype(BF16)


def _diff_flash_kernel(q_ref, k_ref, v_ref, lq_ref, lk_ref, og_ref, o_ref, qs_scr, m_scr, l_scr, acc_scr, *, lam_init):
    qi, ki = pl.program_id(2), pl.program_id(3)
    tq, tk = q_ref.shape[0], k_ref.shape[0]
    last = ((qi + 1) * tq - 1) // tk

    @pl.when(ki == 0)
    def _():
        m_scr[...] = jnp.full_like(m_scr, -jnp.inf)
        l_scr[...] = jnp.zeros_like(l_scr)
        acc_scr[...] = jnp.zeros_like(acc_scr)
        for g in range(DIFF_GROUP):
            q0, q1 = _component_masks(q_ref[:, g * LANES:(g + 1) * LANES])
            qs_scr[0, g * tq:(g + 1) * tq, :] = q0
            qs_scr[1, g * tq:(g + 1) * tq, :] = q1

    def step(masked):
        kb = k_ref[...].astype(BF16)
        vb = v_ref[...].astype(BF16)
        for c in range(2):
            s = _dot_nt(qs_scr[c], kb)
            if masked:
                rel = lax.broadcasted_iota(jnp.int32, s.shape, 0) % tq - lax.broadcasted_iota(jnp.int32, s.shape, 1)
                s = jnp.where(rel >= ki * tk - qi * tq, s, -jnp.inf)
            _softmax_step(s, vb, m_scr.at[c], l_scr.at[c], acc_scr.at[c])

    visible = (ki + 1) * tk <= qi * tq + 1

    @pl.when(visible)
    def _():
        step(False)

    @pl.when(jnp.logical_not(visible) & (ki <= last))
    def _():
        step(True)

    @pl.when(ki == last)
    def _():
        lam = _diff_lambda(lq_ref, lk_ref, lam_init)
        for g in range(DIFF_GROUP):
            r = slice(g * tq, (g + 1) * tq)
            o = _diff_finish(acc_scr[0, r], l_scr[0, r], acc_scr[1, r], l_scr[1, r], lam, og_ref[...], lam_init)
            o_ref[:, g * LANES:(g + 1) * LANES] = o.astype(BF16)


def _diff_flash(qkv, lam_q, lam_k, o_gain, lam_init, batch, seq):
    tq, tk = _tile(seq, DIFF_FLASH_Q_TILE), _tile(seq, DIFF_FLASH_K_TILE)
    nq, nk = seq // tq, seq // tk
    gw = DIFF_GROUP * LANES
    kcol0 = DIFF_HEADS
    vcol0 = DIFF_HEADS + DIFF_KV_HEADS
    fixed = lambda b, n, i, j: (0, 0)

    def kv_rows(b, i, j):
        return b * nk + jnp.minimum(j, ((i + 1) * tq - 1) // tk)

    return pl.pallas_call(
        functools.partial(_diff_flash_kernel, lam_init=lam_init),
        out_shape=jax.ShapeDtypeStruct((batch * seq, DIFF_HEADS * LANES), BF16),
        grid=(batch, DIFF_KV_HEADS, nq, nk),
        in_specs=[
            pl.BlockSpec((tq, gw), lambda b, n, i, j: (b * nq + i, n)),
            pl.BlockSpec((tk, LANES), lambda b, n, i, j: (kv_rows(b, i, j), kcol0 + n)),
            pl.BlockSpec((tk, LANES), lambda b, n, i, j: (kv_rows(b, i, j), vcol0 + n)),
            pl.BlockSpec((2, DIFF_HEAD_DIM), fixed),
            pl.BlockSpec((2, DIFF_HEAD_DIM), fixed),
            pl.BlockSpec((1, LANES), fixed),
        ],
        out_specs=pl.BlockSpec((tq, gw), lambda b, n, i, j: (b * nq + i, n)),
        scratch_shapes=[
            pltpu.VMEM((2, DIFF_GROUP * tq, LANES), BF16),
            pltpu.VMEM((2, DIFF_GROUP * tq, 1), F32),
            pltpu.VMEM((2, DIFF_GROUP * tq, 1), F32),
            pltpu.VMEM((2, DIFF_GROUP * tq, LANES), F32),
        ],
        compiler_params=_params(("parallel", "parallel", "parallel", "arbitrary")),
        name="diff_flash",
    )(qkv, qkv, qkv, lam_q, lam_k, o_gain.reshape(1, LANES))


def _diff_decode_kernel(pt_ref, *refs, pps, dec_seq, lam_init):
    k_refs = refs[:pps]
    v_refs = refs[pps:2 * pps]
    new_ref, lq_ref, lk_ref, og_ref, o_ref, q_scr, kb_scr, vb_scr, m_scr, l_scr, acc_scr = refs[2 * pps:]
    p = pl.program_id(1)
    hrows = 2 * DIFF_GROUP * dec_seq
    qcols = DIFF_HEADS * LANES
    kvw = DIFF_KV_HEADS * LANES
    page = k_refs[0].shape[2] // DIFF_KV_HEADS

    @pl.when(p == 0)
    def _():
        for n in range(DIFF_KV_HEADS):
            parts = [_component_masks(new_ref[:, (n * DIFF_GROUP + g) * LANES:(n * DIFF_GROUP + g + 1) * LANES])
                     for g in range(DIFF_GROUP)]
            for c in range(2):
                for g in range(DIFF_GROUP):
                    r0 = n * hrows + (c * DIFF_GROUP + g) * dec_seq
                    q_scr[r0:r0 + dec_seq, :] = parts[g][c].astype(F32)
        m_scr[...] = jnp.full_like(m_scr, -jnp.inf)
        l_scr[...] = jnp.zeros_like(l_scr)
        acc_scr[...] = jnp.zeros_like(acc_scr)

    def attend(k_of, v_of, masked):
        s = jnp.concatenate([_dot_nt(q_scr[n * hrows:(n + 1) * hrows, :].astype(BF16), k_of(n))
                             for n in range(DIFF_KV_HEADS)], axis=0)
        if masked:
            qpos = lax.broadcasted_iota(jnp.int32, s.shape, 0) % dec_seq
            s = jnp.where(lax.broadcasted_iota(jnp.int32, s.shape, 1) <= qpos, s, -jnp.inf)
        m_old = m_scr[...]
        m_new = jnp.maximum(m_old, jnp.max(s, axis=-1, keepdims=True))
        alpha = jnp.exp2(m_old - m_new)
        pr = jnp.exp2(s - m_new)
        l_scr[...] = alpha * l_scr[...] + jnp.sum(pr, axis=-1, keepdims=True)
        pv = jnp.concatenate([_dot(pr[n * hrows:(n + 1) * hrows].astype(BF16), v_of(n))
                              for n in range(DIFF_KV_HEADS)], axis=0)
        acc_scr[...] = alpha * acc_scr[...] + pv
        m_scr[...] = m_new

    for j in range(pps):
        for n in range(DIFF_KV_HEADS):
            rows_n = pl.ds(n, page, stride=DIFF_KV_HEADS)
            kb_scr[n, j * page:(j + 1) * page, :] = k_refs[j][0, 0, rows_n, :].astype(BF16)
            vb_scr[n, j * page:(j + 1) * page, :] = v_refs[j][0, 0, rows_n, :].astype(BF16)
    attend(lambda n: kb_scr[n], lambda n: vb_scr[n], False)

    @pl.when(p == pl.num_programs(1) - 1)
    def _():
        zeros = jnp.zeros((LANES - dec_seq, LANES), F32)

        def new_of(col0):
            return lambda n: jnp.concatenate(
                [new_ref[:, col0 + n * LANES:col0 + (n + 1) * LANES], zeros], axis=0).astype(BF16)

        attend(new_of(qcols), new_of(qcols + kvw), True)
        lam = _diff_lambda(lq_ref, lk_ref, lam_init)
        half = DIFF_GROUP * dec_seq
        for n in range(DIFF_KV_HEADS):
            r0, r1 = slice(n * hrows, n * hrows + half), slice(n * hrows + half, (n + 1) * hrows)
            o = _diff_finish(acc_scr[r0], l_scr[r0], acc_scr[r1], l_scr[r1], lam, og_ref[...], lam_init)
            for g in range(DIFF_GROUP):
                col = (n * DIFF_GROUP + g) * LANES
                o_ref[:, col:col + LANES] = o[g * dec_seq:(g + 1) * dec_seq]


def _diff_decode(page_table, layer, k_cache, v_cache, qkv, lam_q, lam_k, o_gain, lam_init, n_p, dec_seq):
    dec_batch, n_pages = page_table.shape
    prow = k_cache.shape[2]
    assert dec_seq == SUBLANES and n_p % dec_seq == 0
    pps = _tile(n_pages, DIFF_PAGES_PER_STEP, 1)
    rows = DIFF_KV_HEADS * 2 * DIFF_GROUP * dec_seq
    step_tokens = pps * prow // DIFF_KV_HEADS
    off = n_p // dec_seq
    cols = qkv.shape[1]

    def page_spec(j):
        return pl.BlockSpec((1, 1, prow, LANES), lambda n, p, pt: (layer, pt[n * n_pages + p * pps + j], 0, 0))

    fixed2 = lambda n, p, pt: (0, 0)
    grid_spec = pltpu.PrefetchScalarGridSpec(
        num_scalar_prefetch=1,
        grid=(dec_batch, n_pages // pps),
        in_specs=[page_spec(j) for j in range(pps)] * 2 + [
            pl.BlockSpec((dec_seq, cols), lambda n, p, pt: (off + n, 0)),
            pl.BlockSpec((2, DIFF_HEAD_DIM), fixed2),
            pl.BlockSpec((2, DIFF_HEAD_DIM), fixed2),
            pl.BlockSpec((1, LANES), fixed2),
        ],
        out_specs=pl.BlockSpec((dec_seq, DIFF_HEADS * LANES), lambda n, p, pt: (n, 0)),
        scratch_shapes=[
            pltpu.VMEM((rows, LANES), F32),
            pltpu.VMEM((DIFF_KV_HEADS, step_tokens, LANES), BF16),
            pltpu.VMEM((DIFF_KV_HEADS, step_tokens, LANES), BF16),
            pltpu.VMEM((rows, 1), F32),
            pltpu.VMEM((rows, 1), F32),
            pltpu.VMEM((rows, LANES), F32),
        ],
    )
    return pl.pallas_call(
        functools.partial(_diff_decode_kernel, pps=pps, dec_seq=dec_seq, lam_init=lam_init),
        out_shape=jax.ShapeDtypeStruct((dec_batch * dec_seq, DIFF_HEADS * LANES), F32),
        grid_spec=grid_spec,
        compiler_params=_params(("arbitrary", "arbitrary")),
        name="diff_decode",
    )(page_table.reshape(-1), *([k_cache] * pps), *([v_cache] * pps), qkv, lam_q, lam_k, o_gain.reshape(1, LANES))


def _angles(pos, dim, theta):
    inv = jnp.power(jnp.float32(theta), -jnp.arange(0, dim, 2, dtype=F32) / dim)
    ang = pos.astype(F32)[:, None] * inv[None, :]
    return jnp.cos(ang), jnp.sin(ang)


def _mla_rope_tables(pos):
    cos, sin = _angles(pos, MLA_ROPE, MLA_THETA)
    return jnp.concatenate([cos, cos], axis=1), jnp.concatenate([-sin, sin], axis=1)


def _diff_rope_tables(pos):
    cos, sin = _angles(pos, DIFF_ROT, DIFF_THETA)
    n, half = pos.shape[0], DIFF_ROT // 2
    rest = DIFF_HEAD_DIM - DIFF_ROT
    c = jnp.concatenate([cos, cos, jnp.ones((n, rest), F32)], axis=1)
    sa = jnp.concatenate([-sin, jnp.zeros((n, half + rest), F32)], axis=1)
    sb = jnp.concatenate([jnp.zeros((n, half), F32), sin, jnp.zeros((n, rest), F32)], axis=1)
    return tuple(jnp.concatenate([t, t], axis=1) for t in (c, sa, sb))


def kernel(x_prompt, x_sample, state_pool, cache_mla_latent, cache_mla_krope, cache_diff_k, cache_diff_v, page_table,
           norm_mix, norm_mlp, pool_w, pool_scale, mla_w_dq, mla_q_norm, mla_w_uq, mla_w_dkv, mla_kv_norm, mla_w_ukv,
           mla_q_gain, mla_k_gain, mla_w_o, diff_w_qkv, diff_q_gain, diff_k_gain, diff_lam_q, diff_lam_k, diff_o_gain,
           diff_w_o, mlp_w_in, mlp_w_out):
    b, s, d = x_prompt.shape
    db, ds, _ = x_sample.shape
    depth = norm_mix.shape[0]
    page = cache_mla_latent.shape[2]
    past_len = page_table.shape[1] * page
    n_p, n_s = b * s, db * ds
    kvl = mla_kv_norm.shape[1]

    x = jnp.concatenate([x_prompt.reshape(n_p, d), x_sample.reshape(n_s, d)], axis=0)
    pos = jnp.concatenate([jnp.tile(jnp.arange(s), b), jnp.tile(past_len + jnp.arange(ds), db)])
    mla_cos, mla_sin = _mla_rope_tables(pos)
    diff_tabs = _diff_rope_tables(pos)
    w_in_b, w_out_b = mlp_w_in.astype(BF16), mlp_w_out.astype(BF16)

    pool_p, pool_s, lat_p, lat_s, kr_p, kr_s, dk_p, dk_s, dv_p, dv_s = ([] for _ in range(10))
    for i in range(depth):
        kind, j = i % N_MIXERS, i // N_MIXERS
        if kind == 0:
            x, st_p, st_s = _pool_layer(x, state_pool[j], norm_mix[i], pool_w[j].astype(BF16), pool_scale[j],
                                        batch=b, seq=s, dec_batch=db, dec_seq=ds, past_len=past_len)
            pool_p.append(st_p)
            pool_s.append(st_s)
        elif kind == 1:
            cq, lat, kr = _mla_down(x, norm_mix[i], mla_w_dq[j].astype(BF16), mla_w_dkv[j].astype(BF16),
                                    mla_q_norm[j], mla_kv_norm[j], mla_cos, mla_sin)
            ql = cq.shape[1]
            w_uq_h = mla_w_uq[j].astype(BF16).reshape(ql, MLA_HEADS, MLA_QK).transpose(1, 0, 2)
            w_ukv = mla_w_ukv[j].astype(BF16).reshape(kvl, MLA_HEADS, MLA_NOPE + MLA_V)
            w_ukv_h = w_ukv.transpose(1, 0, 2)
            wk_t_h = w_ukv[:, :, :MLA_NOPE].transpose(1, 2, 0)
            q = _mla_q(cq, w_uq_h, mla_cos, mla_sin, mla_q_gain[j])
            k, v = _mla_kv(lat, kr, w_ukv_h, mla_k_gain[j], n_p)
            o_p = _mla_flash(q, k, v, b, s)
            qabs = _mla_absorb(q, wk_t_h, mla_k_gain[j], n_p, n_s)
            olat = _mla_decode(page_table, j, cache_mla_latent, jnp.swapaxes(cache_mla_krope, 2, 3),
                               wk_t_h.reshape(MLA_HEADS * MLA_NOPE, kvl), qabs, q, lat, kr, mla_k_gain[j], n_p, ds)
            o_s = _mla_vexpand(olat, w_ukv_h[:, :, MLA_NOPE:], ds)
            x = _proj_residual(x, jnp.concatenate([o_p, o_s], axis=0), mla_w_o[j].astype(BF16))
            lat_p.append(lat[:n_p].reshape(b, s, kvl))
            lat_s.append(lat[n_p:].reshape(db, ds, kvl))
            kr_p.append(kr[:n_p].reshape(b, s, MLA_ROPE))
            kr_s.append(kr[n_p:].reshape(db, ds, MLA_ROPE))
        else:
            lam_init = 0.8 - 0.6 * math.exp(-0.3 * i)
            qg = jnp.tile((diff_q_gain[j] * (DIFF_HEAD_DIM ** -0.5 * LOG2E)).reshape(1, LANES), (1, DIFF_HEADS))
            kg = jnp.tile(diff_k_gain[j].reshape(1, LANES), (1, DIFF_KV_HEADS))
            gains = jnp.concatenate([qg, kg], axis=1).reshape(-1, 1, DIFF_QKV_BLOCK)
            qkv = _diff_qkv(x, norm_mix[i], diff_w_qkv[j].astype(BF16), gains, *diff_tabs)
            kvw = DIFF_KV_HEADS * LANES
            o_p = _diff_flash(qkv, diff_lam_q[j], diff_lam_k[j], diff_o_gain[j], lam_init, b, s)
            cshape = cache_diff_k.shape[:2] + (page * DIFF_KV_HEADS, LANES)
            o_s = _diff_decode(page_table, j, cache_diff_k.reshape(cshape), cache_diff_v.reshape(cshape),
                               qkv, diff_lam_q[j], diff_lam_k[j], diff_o_gain[j], lam_init, n_p, ds)
            x = _proj_residual(x, jnp.concatenate([o_p, o_s.astype(BF16)], axis=0), diff_w_o[j].astype(BF16))
            qc = DIFF_HEADS * LANES
            kk, vv = qkv[:, qc:qc + kvw], qkv[:, qc + kvw:]
            dk_p.append(kk[:n_p].reshape(b, s, DIFF_KV_HEADS, LANES))
            dk_s.append(kk[n_p:].reshape(db, ds, DIFF_KV_HEADS, LANES))
            dv_p.append(vv[:n_p].reshape(b, s, DIFF_KV_HEADS, LANES))
            dv_s.append(vv[n_p:].reshape(db, ds, DIFF_KV_HEADS, LANES))
        x = _mlp(x, norm_mlp[i], w_in_b, w_out_b, i)

    return (x[:n_p].reshape(b, s, d), x[n_p:].reshape(db, ds, d), jnp.stack(pool_p), jnp.stack(pool_s),
            jnp.stack(lat_p), jnp.stack(lat_s), jnp.stack(kr_p), jnp.stack(kr_s), jnp.stack(dk_p), jnp.stack(dk_s),
            jnp.stack(dv_p), jnp.stack(dv_s))
```

```python
import functools
import math

import jax
import jax.numpy as jnp
from jax import lax
from jax.experimental import pallas as pl
from jax.experimental.pallas import tpu as pltpu

F32 = jnp.float32
BF16 = jnp.bfloat16

EPS = 1e-6
POOL_WINDOWS = (2, 4, 8, 16)
POOL_STATE = 15
POOL_HALO = 16
MLA_HEADS = 16
MLA_NOPE = 128
MLA_ROPE = 64
MLA_V = 128
MLA_QK = MLA_NOPE + MLA_ROPE
MLA_THETA = 10000.0
DIFF_HEADS = 16
DIFF_KV_HEADS = 4
DIFF_GROUP = DIFF_HEADS // DIFF_KV_HEADS
DIFF_HEAD_DIM = 64
DIFF_ROT = DIFF_HEAD_DIM // 4
DIFF_THETA = 500000.0
N_MIXERS = 3

LANES = 128
SUBLANES = 8
VMEM_LIMIT_BYTES = 56 * 1024 * 1024

ROW_TILE = 512
MLP_ROW_TILE = 768
POOL_TILE = 256
FFN_TILE = 1024
MLA_FLASH_Q_TILE = 512
MLA_FLASH_K_TILE = 2048
DIFF_FLASH_Q_TILE = 256
DIFF_FLASH_K_TILE = 1024
PAGES_PER_STEP = 16
DIFF_PAGES_PER_STEP = 16
CHUNK_PAGES = 2
HEADS_PER_STEP = 4
FLASH_HEADS_PER_STEP = 2
SOFTMAX_ROWS = 16
DIFF_QKV_BLOCK = DIFF_KV_HEADS * LANES

LOG2E = math.log2(math.e)


def _tile(n, pref, mult=16):
    if n <= pref:
        return n
    for t in range(pref, 0, -1):
        if n % t == 0 and t % mult == 0:
            return t
    raise ValueError(f"no tile for {n} (pref {pref}, mult {mult})")


def _params(sem):
    return pltpu.CompilerParams(dimension_semantics=sem, vmem_limit_bytes=VMEM_LIMIT_BYTES)


def _rms(x, g, n=None):
    ms = jnp.sum(x * x, axis=-1, keepdims=True) * (1.0 / (n or x.shape[-1]))
    return x * lax.rsqrt(ms + EPS) * g


def _dot(a, b):
    return jnp.dot(a, b, preferred_element_type=F32)


def _dot_nt(a, b):
    return lax.dot_general(a, b, (((1,), (1,)), ((), ())), preferred_element_type=F32)


def _swap_halves(r):
    half = r.shape[-1] // 2
    return jnp.concatenate([r[:, half:], r[:, :half]], axis=-1)


def _mlp_kernel(x_ref, g_ref, win_ref, wout_ref, o_ref, h_scr, acc_scr):
    f = pl.program_id(1)

    @pl.when(f == 0)
    def _():
        h_scr[...] = _rms(x_ref[...], g_ref[...]).astype(BF16)
        acc_scr[...] = jnp.zeros_like(acc_scr)

    a = jnp.maximum(_dot(h_scr[...], win_ref[0]), 0.0)
    acc_scr[...] += _dot((a * a).astype(BF16), wout_ref[0])

    @pl.when(f == pl.num_programs(1) - 1)
    def _():
        o_ref[...] = x_ref[...] + acc_scr[...]


def _mlp(x, g, w_in, w_out, layer):
    nt, d = x.shape
    ffn = w_in.shape[2]
    tm, tf = _tile(nt, MLP_ROW_TILE), _tile(ffn, FFN_TILE, LANES)
    return pl.pallas_call(
        _mlp_kernel,
        out_shape=jax.ShapeDtypeStruct((nt, d), F32),
        grid=(nt // tm, ffn // tf),
        in_specs=[
            pl.BlockSpec((tm, d), lambda i, f: (i, 0)),
            pl.BlockSpec((1, d), lambda i, f: (0, 0)),
            pl.BlockSpec((1, d, tf), lambda i, f: (layer, 0, f)),
            pl.BlockSpec((1, tf, d), lambda i, f: (layer, f, 0)),
        ],
        out_specs=pl.BlockSpec((tm, d), lambda i, f: (i, 0)),
        scratch_shapes=[pltpu.VMEM((tm, d), BF16), pltpu.VMEM((tm, d), F32)],
        compiler_params=_params(("parallel", "arbitrary")),
        name="mlp",
    )(x, g.reshape(1, d), w_in, w_out)


def _proj_kernel(x_ref, a_ref, w_ref, o_ref):
    o_ref[...] = x_ref[...] + _dot(a_ref[...], w_ref[...])


def _proj_residual(x, a, w):
    nt, d = x.shape
    k = a.shape[1]
    tm = _tile(nt, ROW_TILE)
    return pl.pallas_call(
        _proj_kernel,
        out_shape=jax.ShapeDtypeStruct((nt, d), F32),
        grid=(nt // tm,),
        in_specs=[
            pl.BlockSpec((tm, d), lambda i: (i, 0)),
            pl.BlockSpec((tm, k), lambda i: (i, 0)),
            pl.BlockSpec((k, d), lambda i: (0, 0)),
        ],
        out_specs=pl.BlockSpec((tm, d), lambda i: (i, 0)),
        compiler_params=_params(("parallel",)),
        name="proj_residual",
    )(x, a, w)


def _pool_kernel(x_ref, halo_ref, st_ref, g_ref, w_ref, sc_ref, o_ref, tail_ref, sto_ref, ext_scr, exts_scr,
                 *, n_ptiles, tiles_per_seq, dec_seq, past_len):
    i = pl.program_id(0)
    t = x_ref.shape[0]
    gdim = w_ref.shape[1]
    g = g_ref[...]
    xm = x_ref[...]
    hm = _rms(xm, g)
    row = lax.broadcasted_iota(jnp.int32, (t, 1), 0)

    def mix(gi, wsum, cnt):
        c0 = gi * gdim
        delta = wsum / cnt - hm[:, c0:c0 + gdim]
        y = _dot(delta.astype(BF16), w_ref[gi]) * sc_ref[:, c0:c0 + gdim]
        o_ref[:, c0:c0 + gdim] = xm[:, c0:c0 + gdim] + y

    @pl.when(i < n_ptiles)
    def _():
        first = (i % tiles_per_seq) == 0
        hh = _rms(halo_ref[...], g)
        ext_scr[0:POOL_HALO, :] = jnp.where(first, 0.0, hh)
        ext_scr[POOL_HALO:POOL_HALO + t, :] = hm
        tail_ref[0] = hm[t - POOL_HALO:t, :]
        pos = (i % tiles_per_seq) * t + row
        for gi, w in enumerate(POOL_WINDOWS):
            c0 = gi * gdim
            wsum = hm[:, c0:c0 + gdim]
            for k in range(1, w):
                wsum = wsum + ext_scr[POOL_HALO - k:POOL_HALO - k + t, c0:c0 + gdim]
            mix(gi, wsum, jnp.minimum(pos + 1, w).astype(F32))

    @pl.when(i >= n_ptiles)
    def _():
        nseq = t // dec_seq
        hm3 = hm.reshape(nseq, dec_seq, hm.shape[1])
        exts_scr[:, 0:POOL_HALO, :] = st_ref[...]
        exts_scr[:, POOL_HALO:POOL_HALO + dec_seq, :] = hm3
        sto_ref[...] = exts_scr[:, dec_seq:dec_seq + POOL_HALO, :]
        pos = past_len + row % dec_seq
        for gi, w in enumerate(POOL_WINDOWS):
            c0 = gi * gdim
            wsum = hm3[:, :, c0:c0 + gdim]
            for k in range(1, w):
                wsum = wsum + exts_scr[:, POOL_HALO - k:POOL_HALO - k + dec_seq, c0:c0 + gdim]
            mix(gi, wsum.reshape(t, gdim), jnp.minimum(pos + 1, w).astype(F32))


def _pool_layer(x, state, g, w_grp, scale, *, batch, seq, dec_batch, dec_seq, past_len):
    nt, d = x.shape
    assert dec_seq == SUBLANES and POOL_HALO == 2 * dec_seq
    n_p, n_s = batch * seq, dec_batch * dec_seq
    t = _tile(math.gcd(seq, n_s), POOL_TILE)
    n_ptiles, n_stiles, tps = n_p // t, n_s // t, seq // t
    nseq = t // dec_seq
    state16 = jnp.pad(state, ((0, 0), (POOL_HALO - POOL_STATE, 0), (0, 0)))
    hb = t // POOL_HALO
    kern = functools.partial(_pool_kernel, n_ptiles=n_ptiles, tiles_per_seq=tps, dec_seq=dec_seq, past_len=past_len)
    xo, tail, sto = pl.pallas_call(
        kern,
        out_shape=(
            jax.ShapeDtypeStruct((nt, d), F32),
            jax.ShapeDtypeStruct((batch, POOL_HALO, d), F32),
            jax.ShapeDtypeStruct((dec_batch, POOL_HALO, d), F32),
        ),
        grid=(n_ptiles + n_stiles,),
        in_specs=[
            pl.BlockSpec((t, d), lambda i: (i, 0)),
            pl.BlockSpec((POOL_HALO, d), lambda i: (jnp.maximum(i * hb - 1, 0), 0)),
            pl.BlockSpec((nseq, POOL_HALO, d), lambda i: (jnp.maximum(i - n_ptiles, 0), 0, 0)),
            pl.BlockSpec((1, d), lambda i: (0, 0)),
            pl.BlockSpec(w_grp.shape, lambda i: (0, 0, 0)),
            pl.BlockSpec((1, d), lambda i: (0, 0)),
        ],
        out_specs=(
            pl.BlockSpec((t, d), lambda i: (i, 0)),
            pl.BlockSpec((1, POOL_HALO, d), lambda i: (jnp.minimum(i // tps, batch - 1), 0, 0)),
            pl.BlockSpec((nseq, POOL_HALO, d), lambda i: (jnp.maximum(i - n_ptiles, 0), 0, 0)),
        ),
        scratch_shapes=[pltpu.VMEM((t + POOL_HALO, d), F32), pltpu.VMEM((nseq, POOL_HALO + dec_seq, d), F32)],
        compiler_params=_params(("arbitrary",)),
        name="pool_mixer",
    )(x, x, state16, g.reshape(1, d), w_grp, scale.reshape(1, d))
    return xo, tail[:, 1:], sto[:, 1:]


def _mla_down_kernel(x_ref, g_ref, wdq_ref, wdkv_ref, qn_ref, kvn_ref, cos_ref, sin_ref, cq_ref, lat_ref, kr_ref):
    h = _rms(x_ref[...], g_ref[...]).astype(BF16)
    cq_ref[...] = _rms(_dot(h, wdq_ref[...]), qn_ref[...])
    ckr = _dot(h, wdkv_ref[...])
    lora = lat_ref.shape[1]
    lat_ref[...] = _rms(ckr[:, :lora], kvn_ref[...])
    r = ckr[:, lora:]
    kr_ref[...] = r * cos_ref[...] + _swap_halves(r) * sin_ref[...]


def _mla_down(x, g, w_dq, w_dkv, q_norm, kv_norm, cos, sin):
    nt, d = x.shape
    ql, kvl = w_dq.shape[1], kv_norm.shape[0]
    tm = _tile(nt, ROW_TILE)
    row = lambda i: (i, 0)
    fixed = lambda i: (0, 0)
    return pl.pallas_call(
        _mla_down_kernel,
        out_shape=(
            jax.ShapeDtypeStruct((nt, ql), F32),
            jax.ShapeDtypeStruct((nt, kvl), F32),
            jax.ShapeDtypeStruct((nt, MLA_ROPE), F32),
        ),
        grid=(nt // tm,),
        in_specs=[
            pl.BlockSpec((tm, d), row),
            pl.BlockSpec((1, d), fixed),
            pl.BlockSpec(w_dq.shape, fixed),
            pl.BlockSpec(w_dkv.shape, fixed),
            pl.BlockSpec((1, ql), fixed),
            pl.BlockSpec((1, kvl), fixed),
            pl.BlockSpec((tm, MLA_ROPE), row),
            pl.BlockSpec((tm, MLA_ROPE), row),
        ],
        out_specs=(pl.BlockSpec((tm, ql), row), pl.BlockSpec((tm, kvl), row), pl.BlockSpec((tm, MLA_ROPE), row)),
        compiler_params=_params(("parallel",)),
        name="mla_down",
    )(x, g.reshape(1, d), w_dq, w_dkv, q_norm.reshape(1, ql), kv_norm.reshape(1, kvl), cos, sin)


def _mla_q_kernel(cq_ref, w_ref, cos_ref, sin_ref, g_ref, q_ref):
    cq = cq_ref[...].astype(BF16)
    for h in range(w_ref.shape[0]):
        q = _dot(cq, w_ref[h])
        qn = q[:, :MLA_NOPE]
        r = q[:, MLA_NOPE:]
        rr = r * cos_ref[...] + _swap_halves(r) * sin_ref[...]
        ss = jnp.sum(qn * qn, axis=-1, keepdims=True) + jnp.sum(rr * rr, axis=-1, keepdims=True)
        inv = lax.rsqrt(ss * (1.0 / MLA_QK) + EPS) * (MLA_QK ** -0.5 * LOG2E)
        q_ref[h, :, :MLA_NOPE] = qn * inv * g_ref[:, :MLA_NOPE]
        q_ref[h, :, MLA_NOPE:] = rr * inv * g_ref[:, MLA_NOPE:]


def _mla_q(cq, w_uq_h, cos, sin, q_gain):
    nt, ql = cq.shape
    tm = _tile(nt, ROW_TILE)
    hps = HEADS_PER_STEP
    return pl.pallas_call(
        _mla_q_kernel,
        out_shape=jax.ShapeDtypeStruct((MLA_HEADS, nt, MLA_QK), F32),
        grid=(nt // tm, MLA_HEADS // hps),
        in_specs=[
            pl.BlockSpec((tm, ql), lambda i, h: (i, 0)),
            pl.BlockSpec((hps, ql, MLA_QK), lambda i, h: (h, 0, 0)),
            pl.BlockSpec((tm, MLA_ROPE), lambda i, h: (i, 0)),
            pl.BlockSpec((tm, MLA_ROPE), lambda i, h: (i, 0)),
            pl.BlockSpec((1, MLA_QK), lambda i, h: (0, 0)),
        ],
        out_specs=pl.BlockSpec((hps, tm, MLA_QK), lambda i, h: (h, i, 0)),
        compiler_params=_params(("parallel", "arbitrary")),
        name="mla_q",
    )(cq, w_uq_h, cos, sin, q_gain.reshape(1, MLA_QK))


def _mla_kv_kernel(lat_ref, kr_ref, w_ref, g_ref, k_ref, v_ref):
    latb = lat_ref[...].astype(BF16)
    kr = kr_ref[...]
    sskr = jnp.sum(kr * kr, axis=-1, keepdims=True)
    for h in range(w_ref.shape[0]):
        kv = _dot(latb, w_ref[h])
        kn = kv[:, :MLA_NOPE]
        inv = lax.rsqrt((jnp.sum(kn * kn, axis=-1, keepdims=True) + sskr) * (1.0 / MLA_QK) + EPS)
        k_ref[h, :, :MLA_NOPE] = (kn * inv * g_ref[:, :MLA_NOPE]).astype(BF16)
        k_ref[h, :, MLA_NOPE:] = (kr * inv * g_ref[:, MLA_NOPE:]).astype(BF16)
        v_ref[h] = kv[:, MLA_NOPE:].astype(BF16)


def _mla_kv(lat, kr, w_ukv_h, k_gain, n_p):
    kvl = lat.shape[1]
    tm = _tile(n_p, ROW_TILE)
    hps = HEADS_PER_STEP
    return pl.pallas_call(
        _mla_kv_kernel,
        out_shape=(
            jax.ShapeDtypeStruct((MLA_HEADS, n_p, MLA_QK), BF16),
            jax.ShapeDtypeStruct((MLA_HEADS, n_p, MLA_V), BF16),
        ),
        grid=(n_p // tm, MLA_HEADS // hps),
        in_specs=[
            pl.BlockSpec((tm, kvl), lambda i, h: (i, 0)),
            pl.BlockSpec((tm, MLA_ROPE), lambda i, h: (i, 0)),
            pl.BlockSpec((hps, kvl, MLA_NOPE + MLA_V), lambda i, h: (h, 0, 0)),
            pl.BlockSpec((1, MLA_QK), lambda i, h: (0, 0)),
        ],
        out_specs=(
            pl.BlockSpec((hps, tm, MLA_QK), lambda i, h: (h, i, 0)),
            pl.BlockSpec((hps, tm, MLA_V), lambda i, h: (h, i, 0)),
        ),
        compiler_params=_params(("parallel", "arbitrary")),
        name="mla_kv",
    )(lat, kr, w_ukv_h, k_gain.reshape(1, MLA_QK))


def _softmax_step(s, v, m_scr, l_scr, acc_scr):
    m_old = m_scr[...]
    m_new = jnp.maximum(m_old, jnp.max(s, axis=-1, keepdims=True))
    alpha = jnp.exp2(m_old - m_new)
    p = jnp.exp2(s - m_new)
    l_scr[...] = alpha * l_scr[...] + jnp.sum(p, axis=-1, keepdims=True)
    acc_scr[...] = alpha * acc_scr[...] + _dot(p.astype(BF16), v)
    m_scr[...] = m_new


def _softmax_rows(s_scr, p_scr, al_scr, m_scr, l_scr, min_rel=None, row_period=None):
    rows, keys = s_scr.shape
    for r0 in range(0, rows, SOFTMAX_ROWS):
        rs = slice(r0, r0 + SOFTMAX_ROWS)
        s = s_scr[rs, :]
        if min_rel is not None:
            shape = (SOFTMAX_ROWS, keys)
            rel = lax.broadcasted_iota(jnp.int32, shape, 0) - lax.broadcasted_iota(jnp.int32, shape, 1)
            s = jnp.where(rel >= min_rel - r0 % (row_period or rows), s, -jnp.inf)
        m_old = m_scr[rs, :]
        m_new = jnp.maximum(m_old, jnp.max(s, axis=-1, keepdims=True))
        alpha = jnp.exp2(m_old - m_new)
        p = jnp.exp2(s - m_new)
        l_scr[rs, :] = alpha * l_scr[rs, :] + jnp.sum(p, axis=-1, keepdims=True)
        m_scr[rs, :] = m_new
        al_scr[rs, :] = alpha
        p_scr[rs, :] = p.astype(BF16)


def _attend_units(units, min_rel=None, row_period=None):
    for q, k, _, st in units:
        st[0][...] = _dot_nt(q, k)
    for _, _, v, (s_scr, p_scr, al_scr, m_scr, l_scr, acc_scr) in units:
        _softmax_rows(s_scr, p_scr, al_scr, m_scr, l_scr, min_rel, row_period)
        acc_scr[...] = al_scr[...] * acc_scr[...] + _dot(p_scr[...], v)


def _causal_mask(s, q0, k0):
    qi = q0 + lax.broadcasted_iota(jnp.int32, s.shape, 0)
    ki = k0 + lax.broadcasted_iota(jnp.int32, s.shape, 1)
    return jnp.where(qi >= ki, s, -jnp.inf)


def _mla_flash_kernel(q_ref, k_ref, v_ref, o_ref, s_scr, p_scr, al_scr, m_scr, l_scr, acc_scr):
    qi, ki = pl.program_id(2), pl.program_id(3)
    nh, tq, tk = q_ref.shape[0], q_ref.shape[1], k_ref.shape[1]
    last = ((qi + 1) * tq - 1) // tk
    visible = (ki + 1) * tk <= qi * tq + 1

    @pl.when(ki == 0)
    def _():
        m_scr[...] = jnp.full_like(m_scr, -jnp.inf)
        l_scr[...] = jnp.zeros_like(l_scr)
        acc_scr[...] = jnp.zeros_like(acc_scr)

    def units():
        return [(q_ref[h].astype(BF16), k_ref[h], v_ref[h],
                 tuple(r.at[h] for r in (s_scr, p_scr, al_scr, m_scr, l_scr, acc_scr))) for h in range(nh)]

    @pl.when(visible)
    def _():
        _attend_units(units())

    @pl.when(jnp.logical_not(visible) & (ki <= last))
    def _():
        _attend_units(units(), min_rel=ki * tk - qi * tq)

    @pl.when(ki == last)
    def _():
        for h in range(nh):
            o_ref[:, h * MLA_V:(h + 1) * MLA_V] = (acc_scr[h] / l_scr[h]).astype(BF16)


def _mla_flash(q, k, v, batch, seq):
    tq, tk = _tile(seq, MLA_FLASH_Q_TILE), _tile(seq, MLA_FLASH_K_TILE)
    nq, nk = seq // tq, seq // tk
    nh = FLASH_HEADS_PER_STEP

    def kv_rows(b, i, j):
        return b * nk + jnp.minimum(j, ((i + 1) * tq - 1) // tk)

    return pl.pallas_call(
        _mla_flash_kernel,
        out_shape=jax.ShapeDtypeStruct((batch * seq, MLA_HEADS * MLA_V), BF16),
        grid=(batch, MLA_HEADS // nh, nq, nk),
        in_specs=[
            pl.BlockSpec((nh, tq, MLA_QK), lambda b, h, i, j: (h, b * nq + i, 0)),
            pl.BlockSpec((nh, tk, MLA_QK), lambda b, h, i, j: (h, kv_rows(b, i, j), 0)),
            pl.BlockSpec((nh, tk, MLA_V), lambda b, h, i, j: (h, kv_rows(b, i, j), 0)),
        ],
        out_specs=pl.BlockSpec((tq, nh * MLA_V), lambda b, h, i, j: (b * nq + i, h)),
        scratch_shapes=[pltpu.VMEM((nh, tq, tk), F32), pltpu.VMEM((nh, tq, tk), BF16), pltpu.VMEM((nh, tq, 1), F32),
                        pltpu.VMEM((nh, tq, 1), F32), pltpu.VMEM((nh, tq, 1), F32), pltpu.VMEM((nh, tq, MLA_V), F32)],
        compiler_params=_params(("parallel", "parallel", "parallel", "arbitrary")),
        name="mla_flash",
    )(q, k, v)


def _mla_absorb_kernel(q_ref, w_ref, g_ref, o_ref):
    qn = (q_ref[0][:, :MLA_NOPE] * g_ref[:, :MLA_NOPE]).astype(BF16)
    o_ref[0] = _dot(qn, w_ref[0])


def _mla_absorb(q, wk_t_h, k_gain, n_p, n_s):
    kvl = wk_t_h.shape[2]
    ta = math.gcd(n_p, n_s)
    while ta > 1024:
        ta //= 2
    off = n_p // ta
    return pl.pallas_call(
        _mla_absorb_kernel,
        out_shape=jax.ShapeDtypeStruct((MLA_HEADS, n_s, kvl), F32),
        grid=(MLA_HEADS, n_s // ta),
        in_specs=[
            pl.BlockSpec((1, ta, MLA_QK), lambda h, i: (h, off + i, 0)),
            pl.BlockSpec((1, MLA_NOPE, kvl), lambda h, i: (h, 0, 0)),
            pl.BlockSpec((1, MLA_QK), lambda h, i: (0, 0)),
        ],
        out_specs=pl.BlockSpec((1, ta, kvl), lambda h, i: (h, i, 0)),
        compiler_params=_params(("parallel", "arbitrary")),
        name="mla_absorb_q",
    )(q, wk_t_h, k_gain.reshape(1, MLA_QK))


def _page_copies(pt_ref, cache_ref, buf_ref, sem_ref, layer, step, slot):
    pps = buf_ref.shape[1]
    return [pltpu.make_async_copy(cache_ref.at[layer, pt_ref[step * pps + j]], buf_ref.at[slot, j], sem_ref.at[slot])
            for j in range(pps)]


def _gather_step_pages(pt_ref, caches, layer):
    nsteps = pl.num_programs(1)
    g = pl.program_id(0) * nsteps + pl.program_id(1)
    slot = g % 2

    def copies(step, slot_):
        return [c for cache, buf, sem in caches for c in _page_copies(pt_ref, cache, buf, sem, layer, step, slot_)]

    @pl.when(g == 0)
    def _():
        for c in copies(0, 0):
            c.start()

    @pl.when(g + 1 < pl.num_programs(0) * nsteps)
    def _():
        for c in copies(g + 1, 1 - slot):
            c.start()

    for c in copies(g, slot):
        c.wait()
    return slot


def _mla_decode_kernel(pt_ref, lat_hbm, krt_hbm, wkt_ref, qabs_ref, q_ref, nlat_ref, nkr_ref, g_ref, o_ref,
                       lat_buf, krt_buf, lat_sem, krt_sem, lhs_scr, qr_scr, latb_scr, big_scr, m_scr, l_scr, acc_scr,
                       *, layer, dec_seq):
    n, p = pl.program_id(0), pl.program_id(1)
    nk = wkt_ref.shape[0]
    rows = MLA_HEADS * dec_seq
    pps, page = lat_buf.shape[1], lat_buf.shape[2]
    cn = big_scr.shape[2]
    cpages = cn // page
    slot = _gather_step_pages(pt_ref, [(lat_hbm, lat_buf, lat_sem), (krt_hbm, krt_buf, krt_sem)], layer)

    @pl.when((n == 0) & (p == 0))
    def _():
        lhs_scr[0:nk, :] = wkt_ref[...]

    @pl.when(p == 0)
    def _():
        lhs_scr[nk:nk + rows, :] = qabs_ref[...].reshape(rows, qabs_ref.shape[2]).astype(BF16)
        qr = q_ref[...][:, :, MLA_NOPE:].reshape(rows, MLA_ROPE)
        qr_scr[...] = (qr * g_ref[:, MLA_NOPE:]).astype(BF16)
        m_scr[...] = jnp.full_like(m_scr, -jnp.inf)
        l_scr[...] = jnp.zeros_like(l_scr)
        acc_scr[...] = jnp.zeros_like(acc_scr)

    def scores(big, sskr, s_rope):
        nn = big.shape[1]
        kt = big[:nk]
        ssn = jnp.sum((kt * kt).reshape(MLA_HEADS, MLA_NOPE, nn), axis=1)
        inv = lax.rsqrt((ssn + sskr) * (1.0 / MLA_QK) + EPS)
        inv = jnp.broadcast_to(inv[:, None, :], (MLA_HEADS, dec_seq, nn)).reshape(rows, nn)
        return (big[nk:] + s_rope) * inv

    for j in range(pps):
        latb_scr[j * page:(j + 1) * page, :] = lat_buf[slot, j].astype(BF16)
    for c in range(pps // cpages):
        big_scr[c] = _dot_nt(lhs_scr[...], latb_scr[c * cn:(c + 1) * cn, :])
    parts = []
    for c in range(pps // cpages):
        krt = jnp.concatenate([krt_buf[slot, c * cpages + j] for j in range(cpages)], axis=1)
        sskr = jnp.sum(krt * krt, axis=0, keepdims=True)
        parts.append(scores(big_scr[c], sskr, _dot(qr_scr[...], krt.astype(BF16))))
    _softmax_step(jnp.concatenate(parts, axis=1), latb_scr[...], m_scr, l_scr, acc_scr)

    @pl.when(p == pl.num_programs(1) - 1)
    def _():
        pad = LANES - dec_seq
        latn = jnp.concatenate([nlat_ref[...], jnp.zeros((pad, nlat_ref.shape[1]), F32)], axis=0).astype(BF16)
        kr = jnp.concatenate([nkr_ref[...], jnp.zeros((pad, MLA_ROPE), F32)], axis=0)
        sq = kr * kr
        hi = sq.astype(BF16)
        lo = (sq - hi.astype(F32)).astype(BF16)
        ones = jnp.ones((SUBLANES, MLA_ROPE), BF16)
        sskr = (_dot_nt(ones, hi) + _dot_nt(ones, lo))[0:1]
        s = scores(_dot_nt(lhs_scr[...], latn), sskr, _dot_nt(qr_scr[...], kr.astype(BF16)))
        qpos = lax.broadcasted_iota(jnp.int32, s.shape, 0) % dec_seq
        s = jnp.where(lax.broadcasted_iota(jnp.int32, s.shape, 1) <= qpos, s, -jnp.inf)
        _softmax_step(s, latn, m_scr, l_scr, acc_scr)
        o_ref[0] = acc_scr[...] / l_scr[...]


def _mla_decode(page_table, layer, lat_cache, krt_cache, wk_t, qabs, q, lat, kr, k_gain, n_p, dec_seq):
    dec_batch, n_pages = page_table.shape
    page, kvl = lat_cache.shape[2], lat_cache.shape[3]
    assert dec_seq == SUBLANES and n_p % dec_seq == 0
    pps = _tile(n_pages, PAGES_PER_STEP, 1)
    cpages = _tile(pps, CHUNK_PAGES, 1)
    rows = MLA_HEADS * dec_seq
    off = n_p // dec_seq

    fixed2 = lambda n, p, pt: (0, 0)
    grid_spec = pltpu.PrefetchScalarGridSpec(
        num_scalar_prefetch=1,
        grid=(dec_batch, n_pages // pps),
        in_specs=[
            pl.BlockSpec(memory_space=pl.ANY),
            pl.BlockSpec(memory_space=pl.ANY),
            pl.BlockSpec(wk_t.shape, fixed2),
            pl.BlockSpec((MLA_HEADS, dec_seq, kvl), lambda n, p, pt: (0, n, 0)),
            pl.BlockSpec((MLA_HEADS, dec_seq, MLA_QK), lambda n, p, pt: (0, off + n, 0)),
            pl.BlockSpec((dec_seq, kvl), lambda n, p, pt: (off + n, 0)),
            pl.BlockSpec((dec_seq, MLA_ROPE), lambda n, p, pt: (off + n, 0)),
            pl.BlockSpec((1, MLA_QK), fixed2),
        ],
        out_specs=pl.BlockSpec((1, rows, kvl), lambda n, p, pt: (n, 0, 0)),
        scratch_shapes=[
            pltpu.VMEM((2, pps, page, kvl), F32),
            pltpu.VMEM((2, pps, MLA_ROPE, page), F32),
            pltpu.SemaphoreType.DMA((2,)),
            pltpu.SemaphoreType.DMA((2,)),
            pltpu.VMEM((wk_t.shape[0] + rows, kvl), BF16),
            pltpu.VMEM((rows, MLA_ROPE), BF16),
            pltpu.VMEM((pps * page, kvl), BF16),
            pltpu.VMEM((pps // cpages, wk_t.shape[0] + rows, cpages * page), F32),
            pltpu.VMEM((rows, 1), F32),
            pltpu.VMEM((rows, 1), F32),
            pltpu.VMEM((rows, kvl), F32),
        ],
    )
    return pl.pallas_call(
        functools.partial(_mla_decode_kernel, layer=layer, dec_seq=dec_seq),
        out_shape=jax.ShapeDtypeStruct((dec_batch, rows, kvl), F32),
        grid_spec=grid_spec,
        compiler_params=_params(("arbitrary", "arbitrary")),
        name="mla_decode",
    )(page_table.reshape(-1), lat_cache, krt_cache, wk_t, qabs, q, lat, kr, k_gain.reshape(1, MLA_QK))


def _mla_vexpand_kernel(a_ref, w_ref, o_ref):
    a = a_ref[...]
    o_ref[...] = _dot(a.reshape(a.shape[0] * a.shape[2], a.shape[3]).astype(BF16), w_ref[0]).astype(BF16)


def _mla_vexpand(olat, wv_h, dec_seq):
    dec_batch, _, kvl = olat.shape
    n_s = dec_batch * dec_seq
    return pl.pallas_call(
        _mla_vexpand_kernel,
        out_shape=jax.ShapeDtypeStruct((n_s, MLA_HEADS * MLA_V), BF16),
        grid=(MLA_HEADS,),
        in_specs=[
            pl.BlockSpec((dec_batch, 1, dec_seq, kvl), lambda h: (0, h, 0, 0)),
            pl.BlockSpec((1, kvl, MLA_V), lambda h: (h, 0, 0)),
        ],
        out_specs=pl.BlockSpec((n_s, MLA_V), lambda h: (0, h)),
        compiler_params=_params(("parallel",)),
        name="mla_vexpand",
    )(olat.reshape(dec_batch, MLA_HEADS, dec_seq, kvl), wv_h)


def _diff_qkv_kernel(x_ref, g_ref, w_ref, gain_ref, c_ref, sa_ref, sb_ref, o_ref, h_scr, *, n_qk_blocks):
    j = pl.program_id(1)

    @pl.when(j == 0)
    def _():
        h_scr[...] = _rms(x_ref[...], g_ref[...]).astype(BF16)

    y_all = _dot(h_scr[...], w_ref[...])

    @pl.when(j < n_qk_blocks)
    def _():
        half = DIFF_ROT // 2
        for b in range(y_all.shape[1] // LANES):
            cols = slice(b * LANES, (b + 1) * LANES)
            y = y_all[:, cols]
            sq = y * y
            lo = lax.broadcasted_iota(jnp.int32, y.shape, 1) < DIFF_HEAD_DIM
            s_lo = jnp.sum(jnp.where(lo, sq, 0.0), axis=-1, keepdims=True)
            s_hi = jnp.sum(jnp.where(lo, 0.0, sq), axis=-1, keepdims=True)
            inv = lax.rsqrt(jnp.where(lo, s_lo, s_hi) * (1.0 / DIFF_HEAD_DIM) + EPS)
            yn = y * inv * gain_ref[0, :, cols]
            o_ref[:, cols] = (yn * c_ref[...] + pltpu.roll(yn, LANES - half, 1) * sa_ref[...]
                              + pltpu.roll(yn, half, 1) * sb_ref[...])

    @pl.when(j >= n_qk_blocks)
    def _():
        o_ref[...] = y_all


def _diff_qkv(x, g, w_qkv, gains, c_tab, sa_tab, sb_tab):
    nt, d = x.shape
    cols = w_qkv.shape[1]
    tn = DIFF_QKV_BLOCK
    n_blocks = cols // tn
    n_qk_blocks = gains.shape[0]
    tm = _tile(nt, ROW_TILE)
    row = lambda i, j: (i, 0)
    return pl.pallas_call(
        functools.partial(_diff_qkv_kernel, n_qk_blocks=n_qk_blocks),
        out_shape=jax.ShapeDtypeStruct((nt, cols), F32),
        grid=(nt // tm, n_blocks),
        in_specs=[
            pl.BlockSpec((tm, d), row),
            pl.BlockSpec((1, d), lambda i, j: (0, 0)),
            pl.BlockSpec((d, tn), lambda i, j: (0, j)),
            pl.BlockSpec((1, 1, tn), lambda i, j: (jnp.minimum(j, n_qk_blocks - 1), 0, 0)),
            pl.BlockSpec((tm, LANES), row),
            pl.BlockSpec((tm, LANES), row),
            pl.BlockSpec((tm, LANES), row),
        ],
        out_specs=pl.BlockSpec((tm, tn), lambda i, j: (i, j)),
        scratch_shapes=[pltpu.VMEM((tm, d), BF16)],
        compiler_params=_params(("parallel", "arbitrary")),
        name="diff_qkv",
    )(x, g.reshape(1, d), w_qkv, gains, c_tab, sa_tab, sb_tab)


def _diff_lambda(lq_ref, lk_ref, lam_init):
    e = jnp.exp(jnp.sum(lq_ref[...] * lk_ref[...], axis=-1, keepdims=True))
    return e[0:1] - e[1:2] + lam_init


def _diff_finish(a0, l0, a1, l1, lam, og, lam_init):
    o = a0 / l0 - lam * (a1 / l1)
    return _rms(o, og) * (1.0 - lam_init)


def _component_masks(x):
    lo = lax.broadcasted_iota(jnp.int32, x.shape, 1) < DIFF_HEAD_DIM
    return jnp.where(lo, x, 0.0).astype(BF16), jnp.where(lo, 0.0, x).astype(BF16)


def _diff_flash_kernel(q_ref, k_ref, v_ref, lq_ref, lk_ref, og_ref, o_ref, qs_scr, m_scr, l_scr, acc_scr, *, lam_init):
    qi, ki = pl.program_id(2), pl.program_id(3)
    tq, tk = q_ref.shape[0], k_ref.shape[0]
    last = ((qi + 1) * tq - 1) // tk

    @pl.when(ki == 0)
    def _():
        m_scr[...] = jnp.full_like(m_scr, -jnp.inf)
        l_scr[...] = jnp.zeros_like(l_scr)
        acc_scr[...] = jnp.zeros_like(acc_scr)
        for g in range(DIFF_GROUP):
            q0, q1 = _component_masks(q_ref[:, g * LANES:(g + 1) * LANES])
            qs_scr[0, g * tq:(g + 1) * tq, :] = q0
            qs_scr[1, g * tq:(g + 1) * tq, :] = q1

    def step(masked):
        kb = k_ref[...].astype(BF16)
        vb = v_ref[...].astype(BF16)
        for c in range(2):
            s = _dot_nt(qs_scr[c], kb)
            if masked:
                rel = lax.broadcasted_iota(jnp.int32, s.shape, 0) % tq - lax.broadcasted_iota(jnp.int32, s.shape, 1)
                s = jnp.where(rel >= ki * tk - qi * tq, s, -jnp.inf)
            _softmax_step(s, vb, m_scr.at[c], l_scr.at[c], acc_scr.at[c])

    visible = (ki + 1) * tk <= qi * tq + 1

    @pl.when(visible)
    def _():
        step(False)

    @pl.when(jnp.logical_not(visible) & (ki <= last))
    def _():
        step(True)

    @pl.when(ki == last)
    def _():
        lam = _diff_lambda(lq_ref, lk_ref, lam_init)
        for g in range(DIFF_GROUP):
            r = slice(g * tq, (g + 1) * tq)
            o = _diff_finish(acc_scr[0, r], l_scr[0, r], acc_scr[1, r], l_scr[1, r], lam, og_ref[...], lam_init)
            o_ref[:, g * LANES:(g + 1) * LANES] = o.astype(BF16)


def _diff_flash(qkv, lam_q, lam_k, o_gain, lam_init, batch, seq):
    tq, tk = _tile(seq, DIFF_FLASH_Q_TILE), _tile(seq, DIFF_FLASH_K_TILE)
    nq, nk = seq // tq, seq // tk
    gw = DIFF_GROUP * LANES
    kcol0 = DIFF_HEADS
    vcol0 = DIFF_HEADS + DIFF_KV_HEADS
    fixed = lambda b, n, i, j: (0, 0)

    def kv_rows(b, i, j):
        return b * nk + jnp.minimum(j, ((i + 1) * tq - 1) // tk)

    return pl.pallas_call(
        functools.partial(_diff_flash_kernel, lam_init=lam_init),
        out_shape=jax.ShapeDtypeStruct((batch * seq, DIFF_HEADS * LANES), BF16),
        grid=(batch, DIFF_KV_HEADS, nq, nk),
        in_specs=[
            pl.BlockSpec((tq, gw), lambda b, n, i, j: (b * nq + i, n)),
            pl.BlockSpec((tk, LANES), lambda b, n, i, j: (kv_rows(b, i, j), kcol0 + n)),
            pl.BlockSpec((tk, LANES), lambda b, n, i, j: (kv_rows(b, i, j), vcol0 + n)),
            pl.BlockSpec((2, DIFF_HEAD_DIM), fixed),
            pl.BlockSpec((2, DIFF_HEAD_DIM), fixed),
            pl.BlockSpec((1, LANES), fixed),
        ],
        out_specs=pl.BlockSpec((tq, gw), lambda b, n, i, j: (b * nq + i, n)),
        scratch_shapes=[
            pltpu.VMEM((2, DIFF_GROUP * tq, LANES), BF16),
            pltpu.VMEM((2, DIFF_GROUP * tq, 1), F32),
            pltpu.VMEM((2, DIFF_GROUP * tq, 1), F32),
            pltpu.VMEM((2, DIFF_GROUP * tq, LANES), F32),
        ],
        compiler_params=_params(("parallel", "parallel", "parallel", "arbitrary")),
        name="diff_flash",
    )(qkv, qkv, qkv, lam_q, lam_k, o_gain.reshape(1, LANES))


def _diff_decode_kernel(pt_ref, k_hbm, v_hbm, new_ref, lq_ref, lk_ref, og_ref, o_ref, k_buf, v_buf, k_sem, v_sem,
                        q_scr, kb_scr, vb_scr, m_scr, l_scr, acc_scr, *, layer, dec_seq, lam_init):
    p = pl.program_id(1)
    hrows = 2 * DIFF_GROUP * dec_seq
    qcols = DIFF_HEADS * LANES
    kvw = DIFF_KV_HEADS * LANES
    pps, page = k_buf.shape[1], k_buf.shape[2] // DIFF_KV_HEADS
    slot = _gather_step_pages(pt_ref, [(k_hbm, k_buf, k_sem), (v_hbm, v_buf, v_sem)], layer)

    @pl.when(p == 0)
    def _():
        for n in range(DIFF_KV_HEADS):
            parts = [_component_masks(new_ref[:, (n * DIFF_GROUP + g) * LANES:(n * DIFF_GROUP + g + 1) * LANES])
                     for g in range(DIFF_GROUP)]
            for c in range(2):
                for g in range(DIFF_GROUP):
                    r0 = n * hrows + (c * DIFF_GROUP + g) * dec_seq
                    q_scr[r0:r0 + dec_seq, :] = parts[g][c].astype(F32)
        m_scr[...] = jnp.full_like(m_scr, -jnp.inf)
        l_scr[...] = jnp.zeros_like(l_scr)
        acc_scr[...] = jnp.zeros_like(acc_scr)

    def attend(k_of, v_of, masked):
        s = jnp.concatenate([_dot_nt(q_scr[n * hrows:(n + 1) * hrows, :].astype(BF16), k_of(n))
                             for n in range(DIFF_KV_HEADS)], axis=0)
        if masked:
            qpos = lax.broadcasted_iota(jnp.int32, s.shape, 0) % dec_seq
            s = jnp.where(lax.broadcasted_iota(jnp.int32, s.shape, 1) <= qpos, s, -jnp.inf)
        m_old = m_scr[...]
        m_new = jnp.maximum(m_old, jnp.max(s, axis=-1, keepdims=True))
        alpha = jnp.exp2(m_old - m_new)
        pr = jnp.exp2(s - m_new)
        l_scr[...] = alpha * l_scr[...] + jnp.sum(pr, axis=-1, keepdims=True)
        pv = jnp.concatenate([_dot(pr[n * hrows:(n + 1) * hrows].astype(BF16), v_of(n))
                              for n in range(DIFF_KV_HEADS)], axis=0)
        acc_scr[...] = alpha * acc_scr[...] + pv
        m_scr[...] = m_new

    for j in range(pps):
        for n in range(DIFF_KV_HEADS):
            rows_n = pl.ds(n, page, stride=DIFF_KV_HEADS)
            kb_scr[n, j * page:(j + 1) * page, :] = k_buf[slot, j, rows_n, :].astype(BF16)
            vb_scr[n, j * page:(j + 1) * page, :] = v_buf[slot, j, rows_n, :].astype(BF16)
    attend(lambda n: kb_scr[n], lambda n: vb_scr[n], False)

    @pl.when(p == pl.num_programs(1) - 1)
    def _():
        zeros = jnp.zeros((LANES - dec_seq, LANES), F32)

        def new_of(col0):
            return lambda n: jnp.concatenate(
                [new_ref[:, col0 + n * LANES:col0 + (n + 1) * LANES], zeros], axis=0).astype(BF16)

        attend(new_of(qcols), new_of(qcols + kvw), True)
        lam = _diff_lambda(lq_ref, lk_ref, lam_init)
        half = DIFF_GROUP * dec_seq
        for n in range(DIFF_KV_HEADS):
            r0, r1 = slice(n * hrows, n * hrows + half), slice(n * hrows + half, (n + 1) * hrows)
            o = _diff_finish(acc_scr[r0], l_scr[r0], acc_scr[r1], l_scr[r1], lam, og_ref[...], lam_init)
            for g in range(DIFF_GROUP):
                col = (n * DIFF_GROUP + g) * LANES
                o_ref[:, col:col + LANES] = o[g * dec_seq:(g + 1) * dec_seq]


def _diff_decode(page_table, layer, k_cache, v_cache, qkv, lam_q, lam_k, o_gain, lam_init, n_p, dec_seq):
    dec_batch, n_pages = page_table.shape
    prow = k_cache.shape[2]
    assert dec_seq == SUBLANES and n_p % dec_seq == 0
    pps = _tile(n_pages, DIFF_PAGES_PER_STEP, 1)
    rows = DIFF_KV_HEADS * 2 * DIFF_GROUP * dec_seq
    step_tokens = pps * prow // DIFF_KV_HEADS
    off = n_p // dec_seq
    cols = qkv.shape[1]

    fixed2 = lambda n, p, pt: (0, 0)
    grid_spec = pltpu.PrefetchScalarGridSpec(
        num_scalar_prefetch=1,
        grid=(dec_batch, n_pages // pps),
        in_specs=[
            pl.BlockSpec(memory_space=pl.ANY),
            pl.BlockSpec(memory_space=pl.ANY),
            pl.BlockSpec((dec_seq, cols), lambda n, p, pt: (off + n, 0)),
            pl.BlockSpec((2, DIFF_HEAD_DIM), fixed2),
            pl.BlockSpec((2, DIFF_HEAD_DIM), fixed2),
            pl.BlockSpec((1, LANES), fixed2),
        ],
        out_specs=pl.BlockSpec((dec_seq, DIFF_HEADS * LANES), lambda n, p, pt: (n, 0)),
        scratch_shapes=[
            pltpu.VMEM((2, pps, prow, LANES), F32),
            pltpu.VMEM((2, pps, prow, LANES), F32),
            pltpu.SemaphoreType.DMA((2,)),
            pltpu.SemaphoreType.DMA((2,)),
            pltpu.VMEM((rows, LANES), F32),
            pltpu.VMEM((DIFF_KV_HEADS, step_tokens, LANES), BF16),
            pltpu.VMEM((DIFF_KV_HEADS, step_tokens, LANES), BF16),
            pltpu.VMEM((rows, 1), F32),
            pltpu.VMEM((rows, 1), F32),
            pltpu.VMEM((rows, LANES), F32),
        ],
    )
    return pl.pallas_call(
        functools.partial(_diff_decode_kernel, layer=layer, dec_seq=dec_seq, lam_init=lam_init),
        out_shape=jax.ShapeDtypeStruct((dec_batch * dec_seq, DIFF_HEADS * LANES), F32),
        grid_spec=grid_spec,
        compiler_params=_params(("arbitrary", "arbitrary")),
        name="diff_decode",
    )(page_table.reshape(-1), k_cache, v_cache, qkv, lam_q, lam_k, o_gain.reshape(1, LANES))


def _angles(pos, dim, theta):
    inv = jnp.power(jnp.float32(theta), -jnp.arange(0, dim, 2, dtype=F32) / dim)
    ang = pos.astype(F32)[:, None] * inv[None, :]
    return jnp.cos(ang), jnp.sin(ang)


def _mla_rope_tables(pos):
    cos, sin = _angles(pos, MLA_ROPE, MLA_THETA)
    return jnp.concatenate([cos, cos], axis=1), jnp.concatenate([-sin, sin], axis=1)


def _diff_rope_tables(pos):
    cos, sin = _angles(pos, DIFF_ROT, DIFF_THETA)
    n, half = pos.shape[0], DIFF_ROT // 2
    rest = DIFF_HEAD_DIM - DIFF_ROT
    c = jnp.concatenate([cos, cos, jnp.ones((n, rest), F32)], axis=1)
    sa = jnp.concatenate([-sin, jnp.zeros((n, half + rest), F32)], axis=1)
    sb = jnp.concatenate([jnp.zeros((n, half), F32), sin, jnp.zeros((n, rest), F32)], axis=1)
    return tuple(jnp.concatenate([t, t], axis=1) for t in (c, sa, sb))


def kernel(x_prompt, x_sample, state_pool, cache_mla_latent, cache_mla_krope, cache_diff_k, cache_diff_v, page_table,
           norm_mix, norm_mlp, pool_w, pool_scale, mla_w_dq, mla_q_norm, mla_w_uq, mla_w_dkv, mla_kv_norm, mla_w_ukv,
           mla_q_gain, mla_k_gain, mla_w_o, diff_w_qkv, diff_q_gain, diff_k_gain, diff_lam_q, diff_lam_k, diff_o_gain,
           diff_w_o, mlp_w_in, mlp_w_out):
    b, s, d = x_prompt.shape
    db, ds, _ = x_sample.shape
    depth = norm_mix.shape[0]
    page = cache_mla_latent.shape[2]
    past_len = page_table.shape[1] * page
    n_p, n_s = b * s, db * ds
    kvl = mla_kv_norm.shape[1]

    x = jnp.concatenate([x_prompt.reshape(n_p, d), x_sample.reshape(n_s, d)], axis=0)
    pos = jnp.concatenate([jnp.tile(jnp.arange(s), b), jnp.tile(past_len + jnp.arange(ds), db)])
    mla_cos, mla_sin = _mla_rope_tables(pos)
    diff_tabs = _diff_rope_tables(pos)
    w_in_b, w_out_b = mlp_w_in.astype(BF16), mlp_w_out.astype(BF16)

    pool_p, pool_s, lat_p, lat_s, kr_p, kr_s, dk_p, dk_s, dv_p, dv_s = ([] for _ in range(10))
    for i in range(depth):
        kind, j = i % N_MIXERS, i // N_MIXERS
        if kind == 0:
            x, st_p, st_s = _pool_layer(x, state_pool[j], norm_mix[i], pool_w[j].astype(BF16), pool_scale[j],
                                        batch=b, seq=s, dec_batch=db, dec_seq=ds, past_len=past_len)
            pool_p.append(st_p)
            pool_s.append(st_s)
        elif kind == 1:
            cq, lat, kr = _mla_down(x, norm_mix[i], mla_w_dq[j].astype(BF16), mla_w_dkv[j].astype(BF16),
                                    mla_q_norm[j], mla_kv_norm[j], mla_cos, mla_sin)
            ql = cq.shape[1]
            w_uq_h = mla_w_uq[j].astype(BF16).reshape(ql, MLA_HEADS, MLA_QK).transpose(1, 0, 2)
            w_ukv = mla_w_ukv[j].astype(BF16).reshape(kvl, MLA_HEADS, MLA_NOPE + MLA_V)
            w_ukv_h = w_ukv.transpose(1, 0, 2)
            wk_t_h = w_ukv[:, :, :MLA_NOPE].transpose(1, 2, 0)
            q = _mla_q(cq, w_uq_h, mla_cos, mla_sin, mla_q_gain[j])
            k, v = _mla_kv(lat, kr, w_ukv_h, mla_k_gain[j], n_p)
            o_p = _mla_flash(q, k, v, b, s)
            qabs = _mla_absorb(q, wk_t_h, mla_k_gain[j], n_p, n_s)
            olat = _mla_decode(page_table, j, cache_mla_latent, jnp.swapaxes(cache_mla_krope, 2, 3),
                               wk_t_h.reshape(MLA_HEADS * MLA_NOPE, kvl), qabs, q, lat, kr, mla_k_gain[j], n_p, ds)
            o_s = _mla_vexpand(olat, w_ukv_h[:, :, MLA_NOPE:], ds)
            x = _proj_residual(x, jnp.concatenate([o_p, o_s], axis=0), mla_w_o[j].astype(BF16))
            lat_p.append(lat[:n_p].reshape(b, s, kvl))
            lat_s.append(lat[n_p:].reshape(db, ds, kvl))
            kr_p.append(kr[:n_p].reshape(b, s, MLA_ROPE))
            kr_s.append(kr[n_p:].reshape(db, ds, MLA_ROPE))
        else:
            lam_init = 0.8 - 0.6 * math.exp(-0.3 * i)
            qg = jnp.tile((diff_q_gain[j] * (DIFF_HEAD_DIM ** -0.5 * LOG2E)).reshape(1, LANES), (1, DIFF_HEADS))
            kg = jnp.tile(diff_k_gain[j].reshape(1, LANES), (1, DIFF_KV_HEADS))
            gains = jnp.concatenate([qg, kg], axis=1).reshape(-1, 1, DIFF_QKV_BLOCK)
            qkv = _diff_qkv(x, norm_mix[i], diff_w_qkv[j].astype(BF16), gains, *diff_tabs)
            kvw = DIFF_KV_HEADS * LANES
            o_p = _diff_flash(qkv, diff_lam_q[j], diff_lam_k[j], diff_o_gain[j], lam_init, b, s)
            cshape = cache_diff_k.shape[:2] + (page * DIFF_KV_HEADS, LANES)
            o_s = _diff_decode(page_table, j, cache_diff_k.reshape(cshape), cache_diff_v.reshape(cshape),
                               qkv, diff_lam_q[j], diff_lam_k[j], diff_o_gain[j], lam_init, n_p, ds)
            x = _proj_residual(x, jnp.concatenate([o_p, o_s.astype(BF16)], axis=0), diff_w_o[j].astype(BF16))
            qc = DIFF_HEADS * LANES
            kk, vv = qkv[:, qc:qc + kvw], qkv[:, qc + kvw:]
            dk_p.append(kk[:n_p].reshape(b, s, DIFF_KV_HEADS, LANES))
            dk_s.append(kk[n_p:].reshape(db, ds, DIFF_KV_HEADS, LANES))
            dv_p.append(vv[:n_p].reshape(b, s, DIFF_KV_HEADS, LANES))
            dv_s.append(vv[n_p:].reshape(db, ds, DIFF_KV_HEADS, LANES))
        x = _mlp(x, norm_mlp[i], w_in_b, w_out_b, i)

    return (x[:n_p].reshape(b, s, d), x[n_p:].reshape(db, ds, d), jnp.stack(pool_p), jnp.stack(pool_s),
            jnp.stack(lat_p), jnp.stack(lat_s), jnp.stack(kr_p), jnp.stack(kr_s), jnp.stack(dk_p), jnp.stack(dk_s),
            jnp.stack(dv_p), jnp.stack(dv_s))
```

```python
import functools
import math

import jax
import jax.numpy as jnp
from jax import lax
from jax.experimental import pallas as pl
from jax.experimental.pallas import tpu as pltpu

F32 = jnp.float32
BF16 = jnp.bfloat16

EPS = 1e-6
POOL_WINDOWS = (2, 4, 8, 16)
POOL_STATE = 15
POOL_HALO = 16
MLA_HEADS = 16
MLA_NOPE = 128
MLA_ROPE = 64
MLA_V = 128
MLA_QK = MLA_NOPE + MLA_ROPE
MLA_THETA = 10000.0
DIFF_HEADS = 16
DIFF_KV_HEADS = 4
DIFF_GROUP = DIFF_HEADS // DIFF_KV_HEADS
DIFF_HEAD_DIM = 64
DIFF_ROT = DIFF_HEAD_DIM // 4
DIFF_THETA = 500000.0
N_MIXERS = 3

LANES = 128
SUBLANES = 8
VMEM_LIMIT_BYTES = 56 * 1024 * 1024

ROW_TILE = 512
MLP_ROW_TILE = 768
POOL_TILE = 256
FFN_TILE = 1024
MLA_FLASH_Q_TILE = 512
MLA_FLASH_K_TILE = 2048
DIFF_FLASH_Q_TILE = 256
DIFF_FLASH_K_TILE = 1024
PAGES_PER_STEP = 16
DIFF_PAGES_PER_STEP = 16
CHUNK_PAGES = 2
HEADS_PER_STEP = 4
FLASH_HEADS_PER_STEP = 2
SOFTMAX_ROWS = 16
DIFF_QKV_BLOCK = DIFF_KV_HEADS * LANES

LOG2E = math.log2(math.e)


def _tile(n, pref, mult=16):
    if n <= pref:
        return n
    for t in range(pref, 0, -1):
        if n % t == 0 and t % mult == 0:
            return t
    raise ValueError(f"no tile for {n} (pref {pref}, mult {mult})")


def _params(sem):
    return pltpu.CompilerParams(dimension_semantics=sem, vmem_limit_bytes=VMEM_LIMIT_BYTES)


def _rms(x, g, n=None):
    ms = jnp.sum(x * x, axis=-1, keepdims=True) * (1.0 / (n or x.shape[-1]))
    return x * lax.rsqrt(ms + EPS) * g


def _dot(a, b):
    return jnp.dot(a, b, preferred_element_type=F32)


def _dot_nt(a, b):
    return lax.dot_general(a, b, (((1,), (1,)), ((), ())), preferred_element_type=F32)


def _swap_halves(r):
    half = r.shape[-1] // 2
    return jnp.concatenate([r[:, half:], r[:, :half]], axis=-1)


def _mlp_kernel(x_ref, g_ref, win_ref, wout_ref, o_ref, h_scr, acc_scr):
    f = pl.program_id(1)

    @pl.when(f == 0)
    def _():
        h_scr[...] = _rms(x_ref[...], g_ref[...]).astype(BF16)
        acc_scr[...] = jnp.zeros_like(acc_scr)

    a = jnp.maximum(_dot(h_scr[...], win_ref[0]), 0.0)
    acc_scr[...] += _dot((a * a).astype(BF16), wout_ref[0])

    @pl.when(f == pl.num_programs(1) - 1)
    def _():
        o_ref[...] = x_ref[...] + acc_scr[...]


def _mlp(x, g, w_in, w_out, layer):
    nt, d = x.shape
    ffn = w_in.shape[2]
    tm, tf = _tile(nt, MLP_ROW_TILE), _tile(ffn, FFN_TILE, LANES)
    return pl.pallas_call(
        _mlp_kernel,
        out_shape=jax.ShapeDtypeStruct((nt, d), F32),
        grid=(nt // tm, ffn // tf),
        in_specs=[
            pl.BlockSpec((tm, d), lambda i, f: (i, 0)),
            pl.BlockSpec((1, d), lambda i, f: (0, 0)),
            pl.BlockSpec((1, d, tf), lambda i, f: (layer, 0, f)),
            pl.BlockSpec((1, tf, d), lambda i, f: (layer, f, 0)),
        ],
        out_specs=pl.BlockSpec((tm, d), lambda i, f: (i, 0)),
        scratch_shapes=[pltpu.VMEM((tm, d), BF16), pltpu.VMEM((tm, d), F32)],
        compiler_params=_params(("parallel", "arbitrary")),
        name="mlp",
    )(x, g.reshape(1, d), w_in, w_out)


def _proj_kernel(x_ref, a_ref, w_ref, o_ref):
    o_ref[...] = x_ref[...] + _dot(a_ref[...], w_ref[...])


def _proj_residual(x, a, w):
    nt, d = x.shape
    k = a.shape[1]
    tm = _tile(nt, ROW_TILE)
    return pl.pallas_call(
        _proj_kernel,
        out_shape=jax.ShapeDtypeStruct((nt, d), F32),
        grid=(nt // tm,),
        in_specs=[
            pl.BlockSpec((tm, d), lambda i: (i, 0)),
            pl.BlockSpec((tm, k), lambda i: (i, 0)),
            pl.BlockSpec((k, d), lambda i: (0, 0)),
        ],
        out_specs=pl.BlockSpec((tm, d), lambda i: (i, 0)),
        compiler_params=_params(("parallel",)),
        name="proj_residual",
    )(x, a, w)


def _pool_kernel(x_ref, halo_ref, st_ref, g_ref, w_ref, sc_ref, o_ref, tail_ref, sto_ref, ext_scr, exts_scr,
                 *, n_ptiles, tiles_per_seq, dec_seq, past_len):
    i = pl.program_id(0)
    t = x_ref.shape[0]
    gdim = w_ref.shape[1]
    g = g_ref[...]
    xm = x_ref[...]
    hm = _rms(xm, g)
    row = lax.broadcasted_iota(jnp.int32, (t, 1), 0)

    def mix(gi, wsum, cnt):
        c0 = gi * gdim
        delta = wsum / cnt - hm[:, c0:c0 + gdim]
        y = _dot(delta.astype(BF16), w_ref[gi]) * sc_ref[:, c0:c0 + gdim]
        o_ref[:, c0:c0 + gdim] = xm[:, c0:c0 + gdim] + y

    @pl.when(i < n_ptiles)
    def _():
        first = (i % tiles_per_seq) == 0
        hh = _rms(halo_ref[...], g)
        ext_scr[0:POOL_HALO, :] = jnp.where(first, 0.0, hh)
        ext_scr[POOL_HALO:POOL_HALO + t, :] = hm
        tail_ref[0] = hm[t - POOL_HALO:t, :]
        pos = (i % tiles_per_seq) * t + row
        for gi, w in enumerate(POOL_WINDOWS):
            c0 = gi * gdim
            wsum = hm[:, c0:c0 + gdim]
            for k in range(1, w):
                wsum = wsum + ext_scr[POOL_HALO - k:POOL_HALO - k + t, c0:c0 + gdim]
            mix(gi, wsum, jnp.minimum(pos + 1, w).astype(F32))

    @pl.when(i >= n_ptiles)
    def _():
        nseq = t // dec_seq
        hm3 = hm.reshape(nseq, dec_seq, hm.shape[1])
        exts_scr[:, 0:POOL_HALO, :] = st_ref[...]
        exts_scr[:, POOL_HALO:POOL_HALO + dec_seq, :] = hm3
        sto_ref[...] = exts_scr[:, dec_seq:dec_seq + POOL_HALO, :]
        pos = past_len + row % dec_seq
        for gi, w in enumerate(POOL_WINDOWS):
            c0 = gi * gdim
            wsum = hm3[:, :, c0:c0 + gdim]
            for k in range(1, w):
                wsum = wsum + exts_scr[:, POOL_HALO - k:POOL_HALO - k + dec_seq, c0:c0 + gdim]
            mix(gi, wsum.reshape(t, gdim), jnp.minimum(pos + 1, w).astype(F32))


def _pool_layer(x, state, g, w_grp, scale, *, batch, seq, dec_batch, dec_seq, past_len):
    nt, d = x.shape
    assert dec_seq == SUBLANES and POOL_HALO == 2 * dec_seq
    n_p, n_s = batch * seq, dec_batch * dec_seq
    t = _tile(math.gcd(seq, n_s), POOL_TILE)
    n_ptiles, n_stiles, tps = n_p // t, n_s // t, seq // t
    nseq = t // dec_seq
    state16 = jnp.pad(state, ((0, 0), (POOL_HALO - POOL_STATE, 0), (0, 0)))
    hb = t // POOL_HALO
    kern = functools.partial(_pool_kernel, n_ptiles=n_ptiles, tiles_per_seq=tps, dec_seq=dec_seq, past_len=past_len)
    xo, tail, sto = pl.pallas_call(
        kern,
        out_shape=(
            jax.ShapeDtypeStruct((nt, d), F32),
            jax.ShapeDtypeStruct((batch, POOL_HALO, d), F32),
            jax.ShapeDtypeStruct((dec_batch, POOL_HALO, d), F32),
        ),
        grid=(n_ptiles + n_stiles,),
        in_specs=[
            pl.BlockSpec((t, d), lambda i: (i, 0)),
            pl.BlockSpec((POOL_HALO, d), lambda i: (jnp.maximum(i * hb - 1, 0), 0)),
            pl.BlockSpec((nseq, POOL_HALO, d), lambda i: (jnp.maximum(i - n_ptiles, 0), 0, 0)),
            pl.BlockSpec((1, d), lambda i: (0, 0)),
            pl.BlockSpec(w_grp.shape, lambda i: (0, 0, 0)),
            pl.BlockSpec((1, d), lambda i: (0, 0)),
        ],
        out_specs=(
            pl.BlockSpec((t, d), lambda i: (i, 0)),
            pl.BlockSpec((1, POOL_HALO, d), lambda i: (jnp.minimum(i // tps, batch - 1), 0, 0)),
            pl.BlockSpec((nseq, POOL_HALO, d), lambda i: (jnp.maximum(i - n_ptiles, 0), 0, 0)),
        ),
        scratch_shapes=[pltpu.VMEM((t + POOL_HALO, d), F32), pltpu.VMEM((nseq, POOL_HALO + dec_seq, d), F32)],
        compiler_params=_params(("arbitrary",)),
        name="pool_mixer",
    )(x, x, state16, g.reshape(1, d), w_grp, scale.reshape(1, d))
    return xo, tail[:, 1:], sto[:, 1:]


def _mla_down_kernel(x_ref, g_ref, wdq_ref, wdkv_ref, qn_ref, kvn_ref, cos_ref, sin_ref, cq_ref, lat_ref, kr_ref):
    h = _rms(x_ref[...], g_ref[...]).astype(BF16)
    cq_ref[...] = _rms(_dot(h, wdq_ref[...]), qn_ref[...])
    ckr = _dot(h, wdkv_ref[...])
    lora = lat_ref.shape[1]
    lat_ref[...] = _rms(ckr[:, :lora], kvn_ref[...])
    r = ckr[:, lora:]
    kr_ref[...] = r * cos_ref[...] + _swap_halves(r) * sin_ref[...]


def _mla_down(x, g, w_dq, w_dkv, q_norm, kv_norm, cos, sin):
    nt, d = x.shape
    ql, kvl = w_dq.shape[1], kv_norm.shape[0]
    tm = _tile(nt, ROW_TILE)
    row = lambda i: (i, 0)
    fixed = lambda i: (0, 0)
    return pl.pallas_call(
        _mla_down_kernel,
        out_shape=(
            jax.ShapeDtypeStruct((nt, ql), F32),
            jax.ShapeDtypeStruct((nt, kvl), F32),
            jax.ShapeDtypeStruct((nt, MLA_ROPE), F32),
        ),
        grid=(nt // tm,),
        in_specs=[
            pl.BlockSpec((tm, d), row),
            pl.BlockSpec((1, d), fixed),
            pl.BlockSpec(w_dq.shape, fixed),
            pl.BlockSpec(w_dkv.shape, fixed),
            pl.BlockSpec((1, ql), fixed),
            pl.BlockSpec((1, kvl), fixed),
            pl.BlockSpec((tm, MLA_ROPE), row),
            pl.BlockSpec((tm, MLA_ROPE), row),
        ],
        out_specs=(pl.BlockSpec((tm, ql), row), pl.BlockSpec((tm, kvl), row), pl.BlockSpec((tm, MLA_ROPE), row)),
        compiler_params=_params(("parallel",)),
        name="mla_down",
    )(x, g.reshape(1, d), w_dq, w_dkv, q_norm.reshape(1, ql), kv_norm.reshape(1, kvl), cos, sin)


def _mla_q_kernel(cq_ref, w_ref, cos_ref, sin_ref, g_ref, q_ref):
    cq = cq_ref[...].astype(BF16)
    for h in range(w_ref.shape[0]):
        q = _dot(cq, w_ref[h])
        qn = q[:, :MLA_NOPE]
        r = q[:, MLA_NOPE:]
        rr = r * cos_ref[...] + _swap_halves(r) * sin_ref[...]
        ss = jnp.sum(qn * qn, axis=-1, keepdims=True) + jnp.sum(rr * rr, axis=-1, keepdims=True)
        inv = lax.rsqrt(ss * (1.0 / MLA_QK) + EPS) * (MLA_QK ** -0.5 * LOG2E)
        q_ref[h, :, :MLA_NOPE] = qn * inv * g_ref[:, :MLA_NOPE]
        q_ref[h, :, MLA_NOPE:] = rr * inv * g_ref[:, MLA_NOPE:]


def _mla_q(cq, w_uq_h, cos, sin, q_gain):
    nt, ql = cq.shape
    tm = _tile(nt, ROW_TILE)
    hps = HEADS_PER_STEP
    return pl.pallas_call(
        _mla_q_kernel,
        out_shape=jax.ShapeDtypeStruct((MLA_HEADS, nt, MLA_QK), F32),
        grid=(nt // tm, MLA_HEADS // hps),
        in_specs=[
            pl.BlockSpec((tm, ql), lambda i, h: (i, 0)),
            pl.BlockSpec((hps, ql, MLA_QK), lambda i, h: (h, 0, 0)),
            pl.BlockSpec((tm, MLA_ROPE), lambda i, h: (i, 0)),
            pl.BlockSpec((tm, MLA_ROPE), lambda i, h: (i, 0)),
            pl.BlockSpec((1, MLA_QK), lambda i, h: (0, 0)),
        ],
        out_specs=pl.BlockSpec((hps, tm, MLA_QK), lambda i, h: (h, i, 0)),
        compiler_params=_params(("parallel", "arbitrary")),
        name="mla_q",
    )(cq, w_uq_h, cos, sin, q_gain.reshape(1, MLA_QK))


def _mla_kv_kernel(lat_ref, kr_ref, w_ref, g_ref, k_ref, v_ref):
    latb = lat_ref[...].astype(BF16)
    kr = kr_ref[...]
    sskr = jnp.sum(kr * kr, axis=-1, keepdims=True)
    for h in range(w_ref.shape[0]):
        kv = _dot(latb, w_ref[h])
        kn = kv[:, :MLA_NOPE]
        inv = lax.rsqrt((jnp.sum(kn * kn, axis=-1, keepdims=True) + sskr) * (1.0 / MLA_QK) + EPS)
        k_ref[h, :, :MLA_NOPE] = (kn * inv * g_ref[:, :MLA_NOPE]).astype(BF16)
        k_ref[h, :, MLA_NOPE:] = (kr * inv * g_ref[:, MLA_NOPE:]).astype(BF16)
        v_ref[h, :, :MLA_V] = kv[:, MLA_NOPE:].astype(BF16)
        v_ref[h, :, MLA_V:] = jnp.ones((kv.shape[0], MLA_V), BF16)


def _mla_kv(lat, kr, w_ukv_h, k_gain, n_p):
    kvl = lat.shape[1]
    tm = _tile(n_p, ROW_TILE)
    hps = HEADS_PER_STEP
    return pl.pallas_call(
        _mla_kv_kernel,
        out_shape=(
            jax.ShapeDtypeStruct((MLA_HEADS, n_p, MLA_QK), BF16),
            jax.ShapeDtypeStruct((MLA_HEADS, n_p, 2 * MLA_V), BF16),
        ),
        grid=(n_p // tm, MLA_HEADS // hps),
        in_specs=[
            pl.BlockSpec((tm, kvl), lambda i, h: (i, 0)),
            pl.BlockSpec((tm, MLA_ROPE), lambda i, h: (i, 0)),
            pl.BlockSpec((hps, kvl, MLA_NOPE + MLA_V), lambda i, h: (h, 0, 0)),
            pl.BlockSpec((1, MLA_QK), lambda i, h: (0, 0)),
        ],
        out_specs=(
            pl.BlockSpec((hps, tm, MLA_QK), lambda i, h: (h, i, 0)),
            pl.BlockSpec((hps, tm, 2 * MLA_V), lambda i, h: (h, i, 0)),
        ),
        compiler_params=_params(("parallel", "arbitrary")),
        name="mla_kv",
    )(lat, kr, w_ukv_h, k_gain.reshape(1, MLA_QK))


def _softmax_step(s, v, m_scr, l_scr, acc_scr):
    m_old = m_scr[...]
    m_new = jnp.maximum(m_old, jnp.max(s, axis=-1, keepdims=True))
    alpha = jnp.exp2(m_old - m_new)
    p = jnp.exp2(s - m_new)
    l_scr[...] = alpha * l_scr[...] + jnp.sum(p, axis=-1, keepdims=True)
    acc_scr[...] = alpha * acc_scr[...] + _dot(p.astype(BF16), v)
    m_scr[...] = m_new


def _softmax_rows(s_scr, p_scr, al_scr, m_scr, l_scr, min_rel=None, row_period=None):
    rows, keys = s_scr.shape
    for r0 in range(0, rows, SOFTMAX_ROWS):
        rs = slice(r0, r0 + SOFTMAX_ROWS)
        s = s_scr[rs, :]
        if min_rel is not None:
            shape = (SOFTMAX_ROWS, keys)
            rel = lax.broadcasted_iota(jnp.int32, shape, 0) - lax.broadcasted_iota(jnp.int32, shape, 1)
            s = jnp.where(rel >= min_rel - r0 % (row_period or rows), s, -jnp.inf)
        m_old = m_scr[rs, :]
        m_new = jnp.maximum(m_old, jnp.max(s, axis=-1, keepdims=True))
        alpha = jnp.exp2(m_old - m_new)
        p = jnp.exp2(s - m_new)
        if l_scr is not None:
            l_scr[rs, :] = alpha * l_scr[rs, :] + jnp.sum(p, axis=-1, keepdims=True)
        m_scr[rs, :] = m_new
        al_scr[rs, :] = alpha
        p_scr[rs, :] = p.astype(BF16)


def _attend_units(units, min_rel=None, row_period=None):
    for q, k, _, st in units:
        st[0][...] = _dot_nt(q, k)
    for _, _, v, (s_scr, p_scr, al_scr, m_scr, l_scr, acc_scr) in units:
        _softmax_rows(s_scr, p_scr, al_scr, m_scr, l_scr, min_rel, row_period)
        acc_scr[...] = al_scr[...] * acc_scr[...] + _dot(p_scr[...], v)


def _causal_mask(s, q0, k0):
    qi = q0 + lax.broadcasted_iota(jnp.int32, s.shape, 0)
    ki = k0 + lax.broadcasted_iota(jnp.int32, s.shape, 1)
    return jnp.where(qi >= ki, s, -jnp.inf)


def _mla_flash_kernel(q_ref, k_ref, v_ref, o_ref, s_scr, p_scr, al_scr, m_scr, acc_scr):
    qi, ki = pl.program_id(2), pl.program_id(3)
    nh, tq, tk = q_ref.shape[0], q_ref.shape[1], k_ref.shape[1]
    last = ((qi + 1) * tq - 1) // tk
    visible = (ki + 1) * tk <= qi * tq + 1

    @pl.when(ki == 0)
    def _():
        m_scr[...] = jnp.full_like(m_scr, -jnp.inf)
        acc_scr[...] = jnp.zeros_like(acc_scr)

    def units():
        return [(q_ref[h].astype(BF16), k_ref[h], v_ref[h],
                 (s_scr.at[h], p_scr.at[h], al_scr.at[h], m_scr.at[h], None, acc_scr.at[h])) for h in range(nh)]

    @pl.when(visible)
    def _():
        _attend_units(units())

    @pl.when(jnp.logical_not(visible) & (ki <= last))
    def _():
        _attend_units(units(), min_rel=ki * tk - qi * tq)

    @pl.when(ki == last)
    def _():
        for h in range(nh):
            o = acc_scr[h, :, :MLA_V] / acc_scr[h, :, MLA_V:MLA_V + 1]
            o_ref[:, h * MLA_V:(h + 1) * MLA_V] = o.astype(BF16)


def _mla_flash(q, k, v, batch, seq):
    tq, tk = _tile(seq, MLA_FLASH_Q_TILE), _tile(seq, MLA_FLASH_K_TILE)
    nq, nk = seq // tq, seq // tk
    nh = FLASH_HEADS_PER_STEP

    def kv_rows(b, i, j):
        return b * nk + jnp.minimum(j, ((i + 1) * tq - 1) // tk)

    return pl.pallas_call(
        _mla_flash_kernel,
        out_shape=jax.ShapeDtypeStruct((batch * seq, MLA_HEADS * MLA_V), BF16),
        grid=(batch, MLA_HEADS // nh, nq, nk),
        in_specs=[
            pl.BlockSpec((nh, tq, MLA_QK), lambda b, h, i, j: (h, b * nq + i, 0)),
            pl.BlockSpec((nh, tk, MLA_QK), lambda b, h, i, j: (h, kv_rows(b, i, j), 0)),
            pl.BlockSpec((nh, tk, 2 * MLA_V), lambda b, h, i, j: (h, kv_rows(b, i, j), 0)),
        ],
        out_specs=pl.BlockSpec((tq, nh * MLA_V), lambda b, h, i, j: (b * nq + i, h)),
        scratch_shapes=[pltpu.VMEM((nh, tq, tk), F32), pltpu.VMEM((nh, tq, tk), BF16), pltpu.VMEM((nh, tq, 1), F32),
                        pltpu.VMEM((nh, tq, 1), F32), pltpu.VMEM((nh, tq, 2 * MLA_V), F32)],
        compiler_params=_params(("parallel", "parallel", "parallel", "arbitrary")),
        name="mla_flash",
    )(q, k, v)


def _mla_absorb_kernel(q_ref, w_ref, g_ref, o_ref):
    qn = (q_ref[0][:, :MLA_NOPE] * g_ref[:, :MLA_NOPE]).astype(BF16)
    o_ref[0] = _dot(qn, w_ref[0])


def _mla_absorb(q, wk_t_h, k_gain, n_p, n_s):
    kvl = wk_t_h.shape[2]
    ta = math.gcd(n_p, n_s)
    while ta > 1024:
        ta //= 2
    off = n_p // ta
    return pl.pallas_call(
        _mla_absorb_kernel,
        out_shape=jax.ShapeDtypeStruct((MLA_HEADS, n_s, kvl), F32),
        grid=(MLA_HEADS, n_s // ta),
        in_specs=[
            pl.BlockSpec((1, ta, MLA_QK), lambda h, i: (h, off + i, 0)),
            pl.BlockSpec((1, MLA_NOPE, kvl), lambda h, i: (h, 0, 0)),
            pl.BlockSpec((1, MLA_QK), lambda h, i: (0, 0)),
        ],
        out_specs=pl.BlockSpec((1, ta, kvl), lambda h, i: (h, i, 0)),
        compiler_params=_params(("parallel", "arbitrary")),
        name="mla_absorb_q",
    )(q, wk_t_h, k_gain.reshape(1, MLA_QK))


def _page_copies(pt_ref, cache_ref, buf_ref, sem_ref, layer, step, slot):
    pps = buf_ref.shape[1]
    return [pltpu.make_async_copy(cache_ref.at[layer, pt_ref[step * pps + j]], buf_ref.at[slot, j], sem_ref.at[slot])
            for j in range(pps)]


def _gather_step_pages(pt_ref, caches, layer):
    nsteps = pl.num_programs(1)
    g = pl.program_id(0) * nsteps + pl.program_id(1)
    slot = g % 2

    def copies(step, slot_):
        return [c for cache, buf, sem in caches for c in _page_copies(pt_ref, cache, buf, sem, layer, step, slot_)]

    @pl.when(g == 0)
    def _():
        for c in copies(0, 0):
            c.start()

    @pl.when(g + 1 < pl.num_programs(0) * nsteps)
    def _():
        for c in copies(g + 1, 1 - slot):
            c.start()

    for c in copies(g, slot):
        c.wait()
    return slot


def _mla_decode_kernel(pt_ref, lat_hbm, krt_hbm, wkt_ref, qabs_ref, q_ref, nlat_ref, nkr_ref, g_ref, o_ref,
                       lat_buf, krt_buf, lat_sem, krt_sem, lhs_scr, qr_scr, latb_scr, big_scr, m_scr, l_scr, acc_scr,
                       *, layer, dec_seq):
    n, p = pl.program_id(0), pl.program_id(1)
    nk = wkt_ref.shape[0]
    rows = MLA_HEADS * dec_seq
    pps, page = lat_buf.shape[1], lat_buf.shape[2]
    cn = big_scr.shape[2]
    cpages = cn // page
    slot = _gather_step_pages(pt_ref, [(lat_hbm, lat_buf, lat_sem), (krt_hbm, krt_buf, krt_sem)], layer)

    @pl.when((n == 0) & (p == 0))
    def _():
        lhs_scr[0:nk, :] = wkt_ref[...]

    @pl.when(p == 0)
    def _():
        lhs_scr[nk:nk + rows, :] = qabs_ref[...].reshape(rows, qabs_ref.shape[2]).astype(BF16)
        qr = q_ref[...][:, :, MLA_NOPE:].reshape(rows, MLA_ROPE)
        qr_scr[...] = (qr * g_ref[:, MLA_NOPE:]).astype(BF16)
        m_scr[...] = jnp.full_like(m_scr, -jnp.inf)
        l_scr[...] = jnp.zeros_like(l_scr)
        acc_scr[...] = jnp.zeros_like(acc_scr)

    def scores(big, sskr, s_rope):
        nn = big.shape[1]
        kt = big[:nk]
        ssn = jnp.sum((kt * kt).reshape(MLA_HEADS, MLA_NOPE, nn), axis=1)
        inv = lax.rsqrt((ssn + sskr) * (1.0 / MLA_QK) + EPS)
        inv = jnp.broadcast_to(inv[:, None, :], (MLA_HEADS, dec_seq, nn)).reshape(rows, nn)
        return (big[nk:] + s_rope) * inv

    for j in range(pps):
        latb_scr[j * page:(j + 1) * page, :] = lat_buf[slot, j].astype(BF16)
    for c in range(pps // cpages):
        big_scr[c] = _dot_nt(lhs_scr[...], latb_scr[c * cn:(c + 1) * cn, :])
    parts = []
    for c in range(pps // cpages):
        krt = jnp.concatenate([krt_buf[slot, c * cpages + j] for j in range(cpages)], axis=1)
        sskr = jnp.sum(krt * krt, axis=0, keepdims=True)
        parts.append(scores(big_scr[c], sskr, _dot(qr_scr[...], krt.astype(BF16))))
    _softmax_step(jnp.concatenate(parts, axis=1), latb_scr[...], m_scr, l_scr, acc_scr)

    @pl.when(p == pl.num_programs(1) - 1)
    def _():
        pad = LANES - dec_seq
        latn = jnp.concatenate([nlat_ref[...], jnp.zeros((pad, nlat_ref.shape[1]), F32)], axis=0).astype(BF16)
        kr = jnp.concatenate([nkr_ref[...], jnp.zeros((pad, MLA_ROPE), F32)], axis=0)
        sq = kr * kr
        hi = sq.astype(BF16)
        lo = (sq - hi.astype(F32)).astype(BF16)
        ones = jnp.ones((SUBLANES, MLA_ROPE), BF16)
        sskr = (_dot_nt(ones, hi) + _dot_nt(ones, lo))[0:1]
        s = scores(_dot_nt(lhs_scr[...], latn), sskr, _dot_nt(qr_scr[...], kr.astype(BF16)))
        qpos = lax.broadcasted_iota(jnp.int32, s.shape, 0) % dec_seq
        s = jnp.where(lax.broadcasted_iota(jnp.int32, s.shape, 1) <= qpos, s, -jnp.inf)
        _softmax_step(s, latn, m_scr, l_scr, acc_scr)
        o_ref[0] = acc_scr[...] / l_scr[...]


def _mla_decode(page_table, layer, lat_cache, krt_cache, wk_t, qabs, q, lat, kr, k_gain, n_p, dec_seq):
    dec_batch, n_pages = page_table.shape
    page, kvl = lat_cache.shape[2], lat_cache.shape[3]
    assert dec_seq == SUBLANES and n_p % dec_seq == 0
    pps = _tile(n_pages, PAGES_PER_STEP, 1)
    cpages = _tile(pps, CHUNK_PAGES, 1)
    rows = MLA_HEADS * dec_seq
    off = n_p // dec_seq

    fixed2 = lambda n, p, pt: (0, 0)
    grid_spec = pltpu.PrefetchScalarGridSpec(
        num_scalar_prefetch=1,
        grid=(dec_batch, n_pages // pps),
        in_specs=[
            pl.BlockSpec(memory_space=pl.ANY),
            pl.BlockSpec(memory_space=pl.ANY),
            pl.BlockSpec(wk_t.shape, fixed2),
            pl.BlockSpec((MLA_HEADS, dec_seq, kvl), lambda n, p, pt: (0, n, 0)),
            pl.BlockSpec((MLA_HEADS, dec_seq, MLA_QK), lambda n, p, pt: (0, off + n, 0)),
            pl.BlockSpec((dec_seq, kvl), lambda n, p, pt: (off + n, 0)),
            pl.BlockSpec((dec_seq, MLA_ROPE), lambda n, p, pt: (off + n, 0)),
            pl.BlockSpec((1, MLA_QK), fixed2),
        ],
        out_specs=pl.BlockSpec((1, rows, kvl), lambda n, p, pt: (n, 0, 0)),
        scratch_shapes=[
            pltpu.VMEM((2, pps, page, kvl), F32),
            pltpu.VMEM((2, pps, MLA_ROPE, page), F32),
            pltpu.SemaphoreType.DMA((2,)),
            pltpu.SemaphoreType.DMA((2,)),
            pltpu.VMEM((wk_t.shape[0] + rows, kvl), BF16),
            pltpu.VMEM((rows, MLA_ROPE), BF16),
            pltpu.VMEM((pps * page, kvl), BF16),
            pltpu.VMEM((pps // cpages, wk_t.shape[0] + rows, cpages * page), F32),
            pltpu.VMEM((rows, 1), F32),
            pltpu.VMEM((rows, 1), F32),
            pltpu.VMEM((rows, kvl), F32),
        ],
    )
    return pl.pallas_call(
        functools.partial(_mla_decode_kernel, layer=layer, dec_seq=dec_seq),
        out_shape=jax.ShapeDtypeStruct((dec_batch, rows, kvl), F32),
        grid_spec=grid_spec,
        compiler_params=_params(("arbitrary", "arbitrary")),
        name="mla_decode",
    )(page_table.reshape(-1), lat_cache, krt_cache, wk_t, qabs, q, lat, kr, k_gain.reshape(1, MLA_QK))


def _mla_vexpand_kernel(a_ref, w_ref, o_ref):
    a = a_ref[...]
    o_ref[...] = _dot(a.reshape(a.shape[0] * a.shape[2], a.shape[3]).astype(BF16), w_ref[0]).astype(BF16)


def _mla_vexpand(olat, wv_h, dec_seq):
    dec_batch, _, kvl = olat.shape
    n_s = dec_batch * dec_seq
    return pl.pallas_call(
        _mla_vexpand_kernel,
        out_shape=jax.ShapeDtypeStruct((n_s, MLA_HEADS * MLA_V), BF16),
        grid=(MLA_HEADS,),
        in_specs=[
            pl.BlockSpec((dec_batch, 1, dec_seq, kvl), lambda h: (0, h, 0, 0)),
            pl.BlockSpec((1, kvl, MLA_V), lambda h: (h, 0, 0)),
        ],
        out_specs=pl.BlockSpec((n_s, MLA_V), lambda h: (0, h)),
        compiler_params=_params(("parallel",)),
        name="mla_vexpand",
    )(olat.reshape(dec_batch, MLA_HEADS, dec_seq, kvl), wv_h)


def _diff_qkv_kernel(x_ref, g_ref, w_ref, gain_ref, c_ref, sa_ref, sb_ref, o_ref, h_scr, *, n_qk_blocks):
    j = pl.program_id(1)

    @pl.when(j == 0)
    def _():
        h_scr[...] = _rms(x_ref[...], g_ref[...]).astype(BF16)

    y_all = _dot(h_scr[...], w_ref[...])

    @pl.when(j < n_qk_blocks)
    def _():
        half = DIFF_ROT // 2
        for b in range(y_all.shape[1] // LANES):
            cols = slice(b * LANES, (b + 1) * LANES)
            y = y_all[:, cols]
            sq = y * y
            lo = lax.broadcasted_iota(jnp.int32, y.shape, 1) < DIFF_HEAD_DIM
            s_lo = jnp.sum(jnp.where(lo, sq, 0.0), axis=-1, keepdims=True)
            s_hi = jnp.sum(jnp.where(lo, 0.0, sq), axis=-1, keepdims=True)
            inv = lax.rsqrt(jnp.where(lo, s_lo, s_hi) * (1.0 / DIFF_HEAD_DIM) + EPS)
            yn = y * inv * gain_ref[0, :, cols]
            o_ref[:, cols] = (yn * c_ref[...] + pltpu.roll(yn, LANES - half, 1) * sa_ref[...]
                              + pltpu.roll(yn, half, 1) * sb_ref[...])

    @pl.when(j >= n_qk_blocks)
    def _():
        o_ref[...] = y_all


def _diff_qkv(x, g, w_qkv, gains, c_tab, sa_tab, sb_tab):
    nt, d = x.shape
    cols = w_qkv.shape[1]
    tn = DIFF_QKV_BLOCK
    n_blocks = cols // tn
    n_qk_blocks = gains.shape[0]
    tm = _tile(nt, ROW_TILE)
    row = lambda i, j: (i, 0)
    return pl.pallas_call(
        functools.partial(_diff_qkv_kernel, n_qk_blocks=n_qk_blocks),
        out_shape=jax.ShapeDtypeStruct((nt, cols), F32),
        grid=(nt // tm, n_blocks),
        in_specs=[
            pl.BlockSpec((tm, d), row),
            pl.BlockSpec((1, d), lambda i, j: (0, 0)),
            pl.BlockSpec((d, tn), lambda i, j: (0, j)),
            pl.BlockSpec((1, 1, tn), lambda i, j: (jnp.minimum(j, n_qk_blocks - 1), 0, 0)),
            pl.BlockSpec((tm, LANES), row),
            pl.BlockSpec((tm, LANES), row),
            pl.BlockSpec((tm, LANES), row),
        ],
        out_specs=pl.BlockSpec((tm, tn), lambda i, j: (i, j)),
        scratch_shapes=[pltpu.VMEM((tm, d), BF16)],
        compiler_params=_params(("parallel", "arbitrary")),
        name="diff_qkv",
    )(x, g.reshape(1, d), w_qkv, gains, c_tab, sa_tab, sb_tab)


def _diff_lambda(lq_ref, lk_ref, lam_init):
    e = jnp.exp(jnp.sum(lq_ref[...] * lk_ref[...], axis=-1, keepdims=True))
    return e[0:1] - e[1:2] + lam_init


def _diff_finish(a0, l0, a1, l1, lam, og, lam_init):
    o = a0 / l0 - lam * (a1 / l1)
    return _rms(o, og) * (1.0 - lam_init)


def _component_masks(x):
    lo = lax.broadcasted_iota(jnp.int32, x.shape, 1) < DIFF_HEAD_DIM
    return jnp.where(lo, x, 0.0).astype(BF16), jnp.where(lo, 0.0, x).astype(BF16)


def _diff_flash_kernel(q_ref, k_ref, v_ref, lq_ref, lk_ref, og_ref, o_ref, qs_scr, m_scr, l_scr, acc_scr, *, lam_init):
    qi, ki = pl.program_id(2), pl.program_id(3)
    tq, tk = q_ref.shape[0], k_ref.shape[0]
    last = ((qi + 1) * tq - 1) // tk

    @pl.when(ki == 0)
    def _():
        m_scr[...] = jnp.full_like(m_scr, -jnp.inf)
        l_scr[...] = jnp.zeros_like(l_scr)
        acc_scr[...] = jnp.zeros_like(acc_scr)
        for g in range(DIFF_GROUP):
            q0, q1 = _component_masks(q_ref[:, g * LANES:(g + 1) * LANES])
            qs_scr[0, g * tq:(g + 1) * tq, :] = q0
            qs_scr[1, g * tq:(g + 1) * tq, :] = q1

    def step(masked):
        kb = k_ref[...].astype(BF16)
        vb = v_ref[...].astype(BF16)
        for c in range(2):
            s = _dot_nt(qs_scr[c], kb)
            if masked:
                rel = lax.broadcasted_iota(jnp.int32, s.shape, 0) % tq - lax.broadcasted_iota(jnp.int32, s.shape, 1)
                s = jnp.where(rel >= ki * tk - qi * tq, s, -jnp.inf)
            _softmax_step(s, vb, m_scr.at[c], l_scr.at[c], acc_scr.at[c])

    visible = (ki + 1) * tk <= qi * tq + 1

    @pl.when(visible)
    def _():
        step(False)

    @pl.when(jnp.logical_not(visible) & (ki <= last))
    def _():
        step(True)

    @pl.when(ki == last)
    def _():
        lam = _diff_lambda(lq_ref, lk_ref, lam_init)
        for g in range(DIFF_GROUP):
            r = slice(g * tq, (g + 1) * tq)
            o = _diff_finish(acc_scr[0, r], l_scr[0, r], acc_scr[1, r], l_scr[1, r], lam, og_ref[...], lam_init)
            o_ref[:, g * LANES:(g + 1) * LANES] = o.astype(BF16)


def _diff_flash(qkv, lam_q, lam_k, o_gain, lam_init, batch, seq):
    tq, tk = _tile(seq, DIFF_FLASH_Q_TILE), _tile(seq, DIFF_FLASH_K_TILE)
    nq, nk = seq // tq, seq // tk
    gw = DIFF_GROUP * LANES
    kcol0 = DIFF_HEADS
    vcol0 = DIFF_HEADS + DIFF_KV_HEADS
    fixed = lambda b, n, i, j: (0, 0)

    def kv_rows(b, i, j):
        return b * nk + jnp.minimum(j, ((i + 1) * tq - 1) // tk)

    return pl.pallas_call(
        functools.partial(_diff_flash_kernel, lam_init=lam_init),
        out_shape=jax.ShapeDtypeStruct((batch * seq, DIFF_HEADS * LANES), BF16),
        grid=(batch, DIFF_KV_HEADS, nq, nk),
        in_specs=[
            pl.BlockSpec((tq, gw), lambda b, n, i, j: (b * nq + i, n)),
            pl.BlockSpec((tk, LANES), lambda b, n, i, j: (kv_rows(b, i, j), kcol0 + n)),
            pl.BlockSpec((tk, LANES), lambda b, n, i, j: (kv_rows(b, i, j), vcol0 + n)),
            pl.BlockSpec((2, DIFF_HEAD_DIM), fixed),
            pl.BlockSpec((2, DIFF_HEAD_DIM), fixed),
            pl.BlockSpec((1, LANES), fixed),
        ],
        out_specs=pl.BlockSpec((tq, gw), lambda b, n, i, j: (b * nq + i, n)),
        scratch_shapes=[
            pltpu.VMEM((2, DIFF_GROUP * tq, LANES), BF16),
            pltpu.VMEM((2, DIFF_GROUP * tq, 1), F32),
            pltpu.VMEM((2, DIFF_GROUP * tq, 1), F32),
            pltpu.VMEM((2, DIFF_GROUP * tq, LANES), F32),
        ],
        compiler_params=_params(("parallel", "parallel", "parallel", "arbitrary")),
        name="diff_flash",
    )(qkv, qkv, qkv, lam_q, lam_k, o_gain.reshape(1, LANES))


def _diff_decode_kernel(pt_ref, k_hbm, v_hbm, new_ref, lq_ref, lk_ref, og_ref, o_ref, k_buf, v_buf, k_sem, v_sem,
                        q_scr, kb_scr, vb_scr, m_scr, l_scr, acc_scr, *, layer, dec_seq, lam_init):
    p = pl.program_id(1)
    hrows = 2 * DIFF_GROUP * dec_seq
    qcols = DIFF_HEADS * LANES
    kvw = DIFF_KV_HEADS * LANES
    pps, page = k_buf.shape[1], k_buf.shape[2] // DIFF_KV_HEADS
    slot = _gather_step_pages(pt_ref, [(k_hbm, k_buf, k_sem), (v_hbm, v_buf, v_sem)], layer)

    @pl.when(p == 0)
    def _():
        for n in range(DIFF_KV_HEADS):
            parts = [_component_masks(new_ref[:, (n * DIFF_GROUP + g) * LANES:(n * DIFF_GROUP + g + 1) * LANES])
                     for g in range(DIFF_GROUP)]
            for c in range(2):
                for g in range(DIFF_GROUP):
                    r0 = n * hrows + (c * DIFF_GROUP + g) * dec_seq
                    q_scr[r0:r0 + dec_seq, :] = parts[g][c].astype(F32)
        m_scr[...] = jnp.full_like(m_scr, -jnp.inf)
        l_scr[...] = jnp.zeros_like(l_scr)
        acc_scr[...] = jnp.zeros_like(acc_scr)

    def attend(k_of, v_of, masked):
        s = jnp.concatenate([_dot_nt(q_scr[n * hrows:(n + 1) * hrows, :].astype(BF16), k_of(n))
                             for n in range(DIFF_KV_HEADS)], axis=0)
        if masked:
            qpos = lax.broadcasted_iota(jnp.int32, s.shape, 0) % dec_seq
            s = jnp.where(lax.broadcasted_iota(jnp.int32, s.shape, 1) <= qpos, s, -jnp.inf)
        m_old = m_scr[...]
        m_new = jnp.maximum(m_old, jnp.max(s, axis=-1, keepdims=True))
        alpha = jnp.exp2(m_old - m_new)
        pr = jnp.exp2(s - m_new)
        l_scr[...] = alpha * l_scr[...] + jnp.sum(pr, axis=-1, keepdims=True)
        pv = jnp.concatenate([_dot(pr[n * hrows:(n + 1) * hrows].astype(BF16), v_of(n))
                              for n in range(DIFF_KV_HEADS)], axis=0)
        acc_scr[...] = alpha * acc_scr[...] + pv
        m_scr[...] = m_new

    for j in range(pps):
        for n in range(DIFF_KV_HEADS):
            rows_n = pl.ds(n, page, stride=DIFF_KV_HEADS)
            kb_scr[n, j * page:(j + 1) * page, :] = k_buf[slot, j, rows_n, :].astype(BF16)
            vb_scr[n, j * page:(j + 1) * page, :] = v_buf[slot, j, rows_n, :].astype(BF16)
    attend(lambda n: kb_scr[n], lambda n: vb_scr[n], False)

    @pl.when(p == pl.num_programs(1) - 1)
    def _():
        zeros = jnp.zeros((LANES - dec_seq, LANES), F32)

        def new_of(col0):
            return lambda n: jnp.concatenate(
                [new_ref[:, col0 + n * LANES:col0 + (n + 1) * LANES], zeros], axis=0).astype(BF16)

        attend(new_of(qcols), new_of(qcols + kvw), True)
        lam = _diff_lambda(lq_ref, lk_ref, lam_init)
        half = DIFF_GROUP * dec_seq
        for n in range(DIFF_KV_HEADS):
            r0, r1 = slice(n * hrows, n * hrows + half), slice(n * hrows + half, (n + 1) * hrows)
            o = _diff_finish(acc_scr[r0], l_scr[r0], acc_scr[r1], l_scr[r1], lam, og_ref[...], lam_init)
            for g in range(DIFF_GROUP):
                col = (n * DIFF_GROUP + g) * LANES
                o_ref[:, col:col + LANES] = o[g * dec_seq:(g + 1) * dec_seq]


def _diff_decode(page_table, layer, k_cache, v_cache, qkv, lam_q, lam_k, o_gain, lam_init, n_p, dec_seq):
    dec_batch, n_pages = page_table.shape
    prow = k_cache.shape[2]
    assert dec_seq == SUBLANES and n_p % dec_seq == 0
    pps = _tile(n_pages, DIFF_PAGES_PER_STEP, 1)
    rows = DIFF_KV_HEADS * 2 * DIFF_GROUP * dec_seq
    step_tokens = pps * prow // DIFF_KV_HEADS
    off = n_p // dec_seq
    cols = qkv.shape[1]

    fixed2 = lambda n, p, pt: (0, 0)
    grid_spec = pltpu.PrefetchScalarGridSpec(
        num_scalar_prefetch=1,
        grid=(dec_batch, n_pages // pps),
        in_specs=[
            pl.BlockSpec(memory_space=pl.ANY),
            pl.BlockSpec(memory_space=pl.ANY),
            pl.BlockSpec((dec_seq, cols), lambda n, p, pt: (off + n, 0)),
            pl.BlockSpec((2, DIFF_HEAD_DIM), fixed2),
            pl.BlockSpec((2, DIFF_HEAD_DIM), fixed2),
            pl.BlockSpec((1, LANES), fixed2),
        ],
        out_specs=pl.BlockSpec((dec_seq, DIFF_HEADS * LANES), lambda n, p, pt: (n, 0)),
        scratch_shapes=[
            pltpu.VMEM((2, pps, prow, LANES), F32),
            pltpu.VMEM((2, pps, prow, LANES), F32),
            pltpu.SemaphoreType.DMA((2,)),
            pltpu.SemaphoreType.DMA((2,)),
            pltpu.VMEM((rows, LANES), F32),
            pltpu.VMEM((DIFF_KV_HEADS, step_tokens, LANES), BF16),
            pltpu.VMEM((DIFF_KV_HEADS, step_tokens, LANES), BF16),
            pltpu.VMEM((rows, 1), F32),
            pltpu.VMEM((rows, 1), F32),
            pltpu.VMEM((rows, LANES), F32),
        ],
    )
    return pl.pallas_call(
        functools.partial(_diff_decode_kernel, layer=layer, dec_seq=dec_seq, lam_init=lam_init),
        out_shape=jax.ShapeDtypeStruct((dec_batch * dec_seq, DIFF_HEADS * LANES), F32),
        grid_spec=grid_spec,
        compiler_params=_params(("arbitrary", "arbitrary")),
        name="diff_decode",
    )(page_table.reshape(-1), k_cache, v_cache, qkv, lam_q, lam_k, o_gain.reshape(1, LANES))


def _angles(pos, dim, theta):
    inv = jnp.power(jnp.float32(theta), -jnp.arange(0, dim, 2, dtype=F32) / dim)
    ang = pos.astype(F32)[:, None] * inv[None, :]
    return jnp.cos(ang), jnp.sin(ang)


def _mla_rope_tables(pos):
    cos, sin = _angles(pos, MLA_ROPE, MLA_THETA)
    return jnp.concatenate([cos, cos], axis=1), jnp.concatenate([-sin, sin], axis=1)


def _diff_rope_tables(pos):
    cos, sin = _angles(pos, DIFF_ROT, DIFF_THETA)
    n, half = pos.shape[0], DIFF_ROT // 2
    rest = DIFF_HEAD_DIM - DIFF_ROT
    c = jnp.concatenate([cos, cos, jnp.ones((n, rest), F32)], axis=1)
    sa = jnp.concatenate([-sin, jnp.zeros((n, half + rest), F32)], axis=1)
    sb = jnp.concatenate([jnp.zeros((n, half), F32), sin, jnp.zeros((n, rest), F32)], axis=1)
    return tuple(jnp.concatenate([t, t], axis=1) for t in (c, sa, sb))


def kernel(x_prompt, x_sample, state_pool, cache_mla_latent, cache_mla_krope, cache_diff_k, cache_diff_v, page_table,
           norm_mix, norm_mlp, pool_w, pool_scale, mla_w_dq, mla_q_norm, mla_w_uq, mla_w_dkv, mla_kv_norm, mla_w_ukv,
           mla_q_gain, mla_k_gain, mla_w_o, diff_w_qkv, diff_q_gain, diff_k_gain, diff_lam_q, diff_lam_k, diff_o_gain,
           diff_w_o, mlp_w_in, mlp_w_out):
    b, s, d = x_prompt.shape
    db, ds, _ = x_sample.shape
    depth = norm_mix.shape[0]
    page = cache_mla_latent.shape[2]
    past_len = page_table.shape[1] * page
    n_p, n_s = b * s, db * ds
    kvl = mla_kv_norm.shape[1]

    x = jnp.concatenate([x_prompt.reshape(n_p, d), x_sample.reshape(n_s, d)], axis=0)
    pos = jnp.concatenate([jnp.tile(jnp.arange(s), b), jnp.tile(past_len + jnp.arange(ds), db)])
    mla_cos, mla_sin = _mla_rope_tables(pos)
    diff_tabs = _diff_rope_tables(pos)
    w_in_b, w_out_b = mlp_w_in.astype(BF16), mlp_w_out.astype(BF16)

    pool_p, pool_s, lat_p, lat_s, kr_p, kr_s, dk_p, dk_s, dv_p, dv_s = ([] for _ in range(10))
    for i in range(depth):
        kind, j = i % N_MIXERS, i // N_MIXERS
        if kind == 0:
            x, st_p, st_s = _pool_layer(x, state_pool[j], norm_mix[i], pool_w[j].astype(BF16), pool_scale[j],
                                        batch=b, seq=s, dec_batch=db, dec_seq=ds, past_len=past_len)
            pool_p.append(st_p)
            pool_s.append(st_s)
        elif kind == 1:
            cq, lat, kr = _mla_down(x, norm_mix[i], mla_w_dq[j].astype(BF16), mla_w_dkv[j].astype(BF16),
                                    mla_q_norm[j], mla_kv_norm[j], mla_cos, mla_sin)
            ql = cq.shape[1]
            w_uq_h = mla_w_uq[j].astype(BF16).reshape(ql, MLA_HEADS, MLA_QK).transpose(1, 0, 2)
            w_ukv = mla_w_ukv[j].astype(BF16).reshape(kvl, MLA_HEADS, MLA_NOPE + MLA_V)
            w_ukv_h = w_ukv.transpose(1, 0, 2)
            wk_t_h = w_ukv[:, :, :MLA_NOPE].transpose(1, 2, 0)
            q = _mla_q(cq, w_uq_h, mla_cos, mla_sin, mla_q_gain[j])
            k, v = _mla_kv(lat, kr, w_ukv_h, mla_k_gain[j], n_p)
            o_p = _mla_flash(q, k, v, b, s)
            qabs = _mla_absorb(q, wk_t_h, mla_k_gain[j], n_p, n_s)
            olat = _mla_decode(page_table, j, cache_mla_latent, jnp.swapaxes(cache_mla_krope, 2, 3),
                               wk_t_h.reshape(MLA_HEADS * MLA_NOPE, kvl), qabs, q, lat, kr, mla_k_gain[j], n_p, ds)
            o_s = _mla_vexpand(olat, w_ukv_h[:, :, MLA_NOPE:], ds)
            x = _proj_residual(x, jnp.concatenate([o_p, o_s], axis=0), mla_w_o[j].astype(BF16))
            lat_p.append(lat[:n_p].reshape(b, s, kvl))
            lat_s.append(lat[n_p:].reshape(db, ds, kvl))
            kr_p.append(kr[:n_p].reshape(b, s, MLA_ROPE))
            kr_s.append(kr[n_p:].reshape(db, ds, MLA_ROPE))
        else:
            lam_init = 0.8 - 0.6 * math.exp(-0.3 * i)
            qg = jnp.tile((diff_q_gain[j] * (DIFF_HEAD_DIM ** -0.5 * LOG2E)).reshape(1, LANES), (1, DIFF_HEADS))
            kg = jnp.tile(diff_k_gain[j].reshape(1, LANES), (1, DIFF_KV_HEADS))
            gains = jnp.concatenate([qg, kg], axis=1).reshape(-1, 1, DIFF_QKV_BLOCK)
            qkv = _diff_qkv(x, norm_mix[i], diff_w_qkv[j].astype(BF16), gains, *diff_tabs)
            kvw = DIFF_KV_HEADS * LANES
            o_p = _diff_flash(qkv, diff_lam_q[j], diff_lam_k[j], diff_o_gain[j], lam_init, b, s)
            cshape = cache_diff_k.shape[:2] + (page * DIFF_KV_HEADS, LANES)
            o_s = _diff_decode(page_table, j, cache_diff_k.reshape(cshape), cache_diff_v.reshape(cshape),
                               qkv, diff_lam_q[j], diff_lam_k[j], diff_o_gain[j], lam_init, n_p, ds)
            x = _proj_residual(x, jnp.concatenate([o_p, o_s.astype(BF16)], axis=0), diff_w_o[j].astype(BF16))
            qc = DIFF_HEADS * LANES
            kk, vv = qkv[:, qc:qc + kvw], qkv[:, qc + kvw:]
            dk_p.append(kk[:n_p].reshape(b, s, DIFF_KV_HEADS, LANES))
            dk_s.append(kk[n_p:].reshape(db, ds, DIFF_KV_HEADS, LANES))
            dv_p.append(vv[:n_p].reshape(b, s, DIFF_KV_HEADS, LANES))
            dv_s.append(vv[n_p:].reshape(db, ds, DIFF_KV_HEADS, LANES))
        x = _mlp(x, norm_mlp[i], w_in_b, w_out_b, i)

    return (x[:n_p].reshape(b, s, d), x[n_p:].reshape(db, ds, d), jnp.stack(pool_p), jnp.stack(pool_s),
            jnp.stack(lat_p), jnp.stack(lat_s), jnp.stack(kr_p), jnp.stack(kr_s), jnp.stack(dk_p), jnp.stack(dk_s),
            jnp.stack(dv_p), jnp.stack(dv_s))
```
